```python
import jax, jax.numpy as jnp
from jax import lax
import numpy as np

D_MODEL = 4096
BATCH = 16
SEQ = 256
DEPTH = 2
DEC_BATCH = 2
DEC_SEQ = 4096
PAST_LEN = 512

GRID_W = 64
HEAD_DIM = 128
N_Q_A = 12
N_KV_A = 4
GQA_GROUP = N_Q_A // N_KV_A
WINDOW = 128
Q_BLOCK = 128
N_HEADS_B = 12
NA_ROWS = 8
NA_COLS = 16
CONV_CH = 1024
CONV_WIDTH = 3
D_FF = 11008
N_MOD = 9
ROPE_BASE = 10000.0
AXIS_DIM = HEAD_DIM // 2
ROPE_PAIRS = AXIS_DIM // 2
EPS = 1e-6
NEG_INF = -1e30
ATTN_SCALE = HEAD_DIM ** -0.5

WIDTH_A_Q = N_Q_A * HEAD_DIM
WIDTH_A_KV = N_KV_A * HEAD_DIM
WIDTH_B = N_HEADS_B * HEAD_DIM
IN_SIZES = (WIDTH_A_Q, WIDTH_A_KV, WIDTH_A_KV, WIDTH_B, WIDTH_B, WIDTH_B, CONV_CH, CONV_CH, CONV_CH)
IN_WIDTH = sum(IN_SIZES)
MIX_OUT = WIDTH_A_Q + WIDTH_B + CONV_CH

kernel_name = 'hybrid_dit_prefix_context_step'


def _normal(key, shape, scale):
    return jax.random.normal(key, shape, jnp.float32) * scale


def _rms(x, g):
    xf = x.astype(jnp.float32)
    y = xf * lax.rsqrt(jnp.mean(xf * xf, axis=-1, keepdims=True) + EPS)
    return (y * g.astype(jnp.float32)).astype(x.dtype)


def _modulate(x, shift, scale):
    return x * (1 + scale) + shift


def _modulation(cvec, mod_w, mod_b):
    m = jax.nn.silu(cvec) @ mod_w + mod_b
    m = m.reshape(m.shape[:-1] + (N_MOD, D_MODEL))
    return [jnp.expand_dims(m[..., i, :], -2) for i in range(N_MOD)]


def _ffn_sublayer(h, shift, scale, gate, g, w13, w2):
    x = _modulate(_rms(h, g), shift, scale)
    a, b = jnp.split(x @ w13, 2, axis=-1)
    return h + 0.5 * gate * ((jax.nn.silu(a) * b) @ w2)


def _softmax_sink(s, sink):
    m = jnp.max(s, axis=-1, keepdims=True)
    if sink is not None:
        m = jnp.maximum(m, sink)
    e = jnp.exp(s - m)
    den = jnp.sum(e, axis=-1, keepdims=True)
    if sink is not None:
        den = den + jnp.exp(sink - m)
    return e / den


def _split_proj(p):
    points = np.cumsum(IN_SIZES)[:-1].tolist()
    return jnp.split(p, points, axis=-1)


def _axial_rope(n):
    t = jnp.arange(n)
    pos = jnp.stack([t // GRID_W, t % GRID_W], axis=-1).astype(jnp.float32)
    inv = ROPE_BASE ** (-jnp.arange(ROPE_PAIRS, dtype=jnp.float32) * 2.0 / AXIS_DIM)
    ang = pos[:, :, None] * inv
    return jnp.cos(ang), jnp.sin(ang)


def _apply_rope(x, cos, sin):
    sh = x.shape
    xf = x.astype(jnp.float32).reshape(sh[:-1] + (2, 2, ROPE_PAIRS))
    lo, hi = xf[..., 0, :], xf[..., 1, :]
    c = cos[None, :, None]
    s = sin[None, :, None]
    out = jnp.stack([lo * c - hi * s, hi * c + lo * s], axis=-2)
    return out.reshape(sh).astype(x.dtype)


def _short_conv(x, gate_b, gate_c, w, b):
    u = gate_c * x
    y = lax.conv_general_dilated(u, w[:, None, :], window_strides=(1,),
                                 padding=[(CONV_WIDTH // 2, CONV_WIDTH // 2)],
                                 dimension_numbers=('NWC', 'WIO', 'NWC'),
                                 feature_group_count=CONV_CH) + b
    return gate_b * y


def _dense_attention(q, k, v, sink):
    B, S, KV, G, d = q.shape
    nb = S // Q_BLOCK
    qb = jnp.moveaxis(q.reshape(B, nb, Q_BLOCK, KV, G, d), 1, 0)
    sink_f = None if sink is None else sink.astype(jnp.float32)[None, :, :, None, None]

    def blk(qi):
        s = jnp.einsum('bqkgd,bskd->bkgqs', qi, k, preferred_element_type=jnp.float32) * ATTN_SCALE
        p = _softmax_sink(s, sink_f).astype(v.dtype)
        return jnp.einsum('bkgqs,bskd->bqkgd', p, v)

    o = lax.map(blk, qb)
    return jnp.moveaxis(o, 0, 1).reshape(B, S, KV * G * d)


def _window_attention(q, k, v, ck, cv, sink):
    B, N = q.shape[0], q.shape[1]
    nb = N // Q_BLOCK
    span = Q_BLOCK + 2 * WINDOW
    pad = ((0, 0), (WINDOW, WINDOW), (0, 0), (0, 0))
    kp = jnp.pad(k, pad)
    vp = jnp.pad(v, pad)
    sink_f = sink.astype(jnp.float32)[None, :, :, None, None]
    rel = jnp.arange(span)[None, :] - jnp.arange(Q_BLOCK)[:, None]
    band = (rel >= 0) & (rel <= 2 * WINDOW)

    def blk(i):
        start = i * Q_BLOCK
        qi = lax.dynamic_slice_in_dim(q, start, Q_BLOCK, axis=1)
        ki = lax.dynamic_slice_in_dim(kp, start, span, axis=1)
        vi = lax.dynamic_slice_in_dim(vp, start, span, axis=1)
        kpos = start - WINDOW + jnp.arange(span)
        valid = band & ((kpos >= 0) & (kpos < N))[None, :]
        s_loc = jnp.einsum('bqkgd,bskd->bkgqs', qi, ki, preferred_element_type=jnp.float32) * ATTN_SCALE
        s_loc = jnp.where(valid, s_loc, NEG_INF)
        s_ctx = jnp.einsum('bqkgd,bskd->bkgqs', qi, ck, preferred_element_type=jnp.float32) * ATTN_SCALE
        p = _softmax_sink(jnp.concatenate([s_loc, s_ctx], axis=-1), sink_f).astype(v.dtype)
        return (jnp.einsum('bkgqs,bskd->bqkgd', p[..., :span], vi)
                + jnp.einsum('bkgqs,bskd->bqkgd', p[..., span:], cv))

    o = lax.map(blk, jnp.arange(nb))
    return jnp.moveaxis(o, 0, 1).reshape(B, N, -1)


def _neighbourhood_attention(q, k, v, ck, cv, rpb):
    B, N, H, d = q.shape
    rows = N // GRID_W
    kr = min(NA_ROWS, rows)
    qg = q.reshape(B, rows, GRID_W, H, d)
    kg = k.reshape(B, rows, GRID_W, H, d)
    vg = v.reshape(B, rows, GRID_W, H, d)
    qc = jnp.arange(GRID_W)
    cs = jnp.clip(qc - NA_COLS // 2, 0, GRID_W - NA_COLS)
    col_ok = (qc[None, :] >= cs[:, None]) & (qc[None, :] < cs[:, None] + NA_COLS)
    dc_idx = jnp.clip(qc[None, :] - qc[:, None], -(NA_COLS - 1), NA_COLS - 1) + NA_COLS - 1
    mask = jnp.broadcast_to(col_ok[:, None, :], (GRID_W, kr, GRID_W)).reshape(GRID_W, kr * GRID_W)
    rpb_f = rpb.astype(jnp.float32)

    def row(r):
        rs = jnp.clip(r - kr // 2, 0, rows - kr)
        qr = lax.dynamic_index_in_dim(qg, r, axis=1, keepdims=False)
        kr_blk = lax.dynamic_slice_in_dim(kg, rs, kr, axis=1).reshape(B, kr * GRID_W, H, d)
        vr_blk = lax.dynamic_slice_in_dim(vg, rs, kr, axis=1).reshape(B, kr * GRID_W, H, d)
        dr_idx = rs + jnp.arange(kr) - r + NA_ROWS - 1
        bias = rpb_f[:, dr_idx[:, None, None], dc_idx[None, :, :]]
        bias = jnp.transpose(bias, (0, 2, 1, 3)).reshape(H, GRID_W, kr * GRID_W)
        s_loc = jnp.einsum('bqhd,bkhd->bhqk', qr, kr_blk, preferred_element_type=jnp.float32) * ATTN_SCALE + bias
        s_loc = jnp.where(mask, s_loc, NEG_INF)
        s_ctx = jnp.einsum('bqhd,bkhd->bhqk', qr, ck, preferred_element_type=jnp.float32) * ATTN_SCALE
        p = _softmax_sink(jnp.concatenate([s_loc, s_ctx], axis=-1), None).astype(v.dtype)
        nl = kr * GRID_W
        return (jnp.einsum('bhqk,bkhd->bqhd', p[..., :nl], vr_blk)
                + jnp.einsum('bhqk,bkhd->bqhd', p[..., nl:], cv))

    o = lax.map(row, jnp.arange(rows))
    return jnp.moveaxis(o, 0, 1).reshape(B, N, H * d)


def _mix_context(u, w_in, w_out, aqn, akn, sink, bqn, bkn, conv_w, conv_b):
    B, S, _ = u.shape
    qa, ka, va, qb, kb, vb, xc, gb, gc = _split_proj(u @ w_in)
    qa = _rms(qa.reshape(B, S, N_Q_A, HEAD_DIM), aqn).reshape(B, S, N_KV_A, GQA_GROUP, HEAD_DIM)
    ka = _rms(ka.reshape(B, S, N_KV_A, HEAD_DIM), akn)
    va = va.reshape(B, S, N_KV_A, HEAD_DIM)
    qb = _rms(qb.reshape(B, S, N_HEADS_B, HEAD_DIM), bqn)[:, :, :, None, :]
    kb = _rms(kb.reshape(B, S, N_HEADS_B, HEAD_DIM), bkn)
    vb = vb.reshape(B, S, N_HEADS_B, HEAD_DIM)
    oa = _dense_attention(qa, ka, va, sink.reshape(N_KV_A, GQA_GROUP))
    ob = _dense_attention(qb, kb, vb, None)
    oc = _short_conv(xc, gb, gc, conv_w, conv_b)
    out = jnp.concatenate([oa, ob, oc], axis=-1) @ w_out
    return out, ka, va, kb, vb


def _mix_latent(u, ck_a, cv_a, ck_b, cv_b, cos, sin, w_in, w_out, aqn, akn, sink, bqn, bkn, rpb, conv_w, conv_b):
    B, N, _ = u.shape
    qa, ka, va, qb, kb, vb, xc, gb, gc = _split_proj(u @ w_in)
    qa = _apply_rope(_rms(qa.reshape(B, N, N_Q_A, HEAD_DIM), aqn), cos, sin)
    qa = qa.reshape(B, N, N_KV_A, GQA_GROUP, HEAD_DIM)
    ka = _apply_rope(_rms(ka.reshape(B, N, N_KV_A, HEAD_DIM), akn), cos, sin)
    va = va.reshape(B, N, N_KV_A, HEAD_DIM)
    qb = _rms(qb.reshape(B, N, N_HEADS_B, HEAD_DIM), bqn)
    kb = _rms(kb.reshape(B, N, N_HEADS_B, HEAD_DIM), bkn)
    vb = vb.reshape(B, N, N_HEADS_B, HEAD_DIM)
    oa = _window_attention(qa, ka, va, ck_a, cv_a, sink.reshape(N_KV_A, GQA_GROUP))
    ob = _neighbourhood_attention(qb, kb, vb, ck_b, cv_b, rpb)
    oc = _short_conv(xc, gb, gc, conv_w, conv_b)
    return jnp.concatenate([oa, ob, oc], axis=-1) @ w_out


def setup_inputs(seed: int = 0) -> dict:
    key = jax.random.key(seed)
    ks = jax.random.split(key, 27)
    D, L = D_MODEL, DEPTH
    return {
        'x_prompt': _normal(ks[0], (BATCH, SEQ, D), 1.0),
        'x_sample': _normal(ks[1], (DEC_BATCH, DEC_SEQ, D), 1.0),
        'cache_a_k': _normal(ks[2], (DEC_BATCH, L, PAST_LEN, N_KV_A, HEAD_DIM), 1.0),
        'cache_a_v': _normal(ks[3], (DEC_BATCH, L, PAST_LEN, N_KV_A, HEAD_DIM), 1.0),
        'cache_b_k': _normal(ks[4], (DEC_BATCH, L, PAST_LEN, N_HEADS_B, HEAD_DIM), 1.0),
        'cache_b_v': _normal(ks[5], (DEC_BATCH, L, PAST_LEN, N_HEADS_B, HEAD_DIM), 1.0),
        'c': _normal(ks[6], (DEC_BATCH, D), 1.0),
        'c_ctx': _normal(ks[7], (D,), 1.0),
        'mod_w': _normal(ks[8], (L, D, N_MOD * D), 0.5 * D ** -0.5),
        'mod_b': _normal(ks[9], (L, N_MOD * D), 0.01),
        'norm_ffn1': 1.0 + _normal(ks[10], (L, D), 0.01),
        'norm_mix': 1.0 + _normal(ks[11], (L, D), 0.01),
        'norm_ffn2': 1.0 + _normal(ks[12], (L, D), 0.01),
        'ffn1_w13': _normal(ks[13], (L, D, 2 * D_FF), D ** -0.5),
        'ffn1_w2': _normal(ks[14], (L, D_FF, D), D_FF ** -0.5),
        'ffn2_w13': _normal(ks[15], (L, D, 2 * D_FF), D ** -0.5),
        'ffn2_w2': _normal(ks[16], (L, D_FF, D), D_FF ** -0.5),
        'w_in': _normal(ks[17], (L, D, IN_WIDTH), D ** -0.5),
        'w_out': _normal(ks[18], (L, MIX_OUT, D), MIX_OUT ** -0.5),
        'a_q_norm': 1.0 + _normal(ks[19], (L, HEAD_DIM), 0.01),
        'a_k_norm': 1.0 + _normal(ks[20], (L, HEAD_DIM), 0.01),
        'a_sink': _normal(ks[21], (L, N_Q_A), 0.5),
        'b_q_norm': 1.0 + _normal(ks[22], (L, HEAD_DIM), 0.01),
        'b_k_norm': 1.0 + _normal(ks[23], (L, HEAD_DIM), 0.01),
        'b_rpb': _normal(ks[24], (L, N_HEADS_B, 2 * NA_ROWS - 1, 2 * NA_COLS - 1), 0.1),
        'conv_w': _normal(ks[25], (L, CONV_WIDTH, CONV_CH), CONV_WIDTH ** -0.5),
        'conv_b': _normal(ks[26], (L, CONV_CH), 0.01),
    }


def reference(x_prompt, x_sample, cache_a_k, cache_a_v, cache_b_k, cache_b_v, c, c_ctx,
              mod_w, mod_b, norm_ffn1, norm_mix, norm_ffn2, ffn1_w13, ffn1_w2, ffn2_w13, ffn2_w2,
              w_in, w_out, a_q_norm, a_k_norm, a_sink, b_q_norm, b_k_norm, b_rpb, conv_w, conv_b):
    cos, sin = _axial_rope(x_sample.shape[1])
    hp, hs = x_prompt, x_sample
    new_ak, new_av, new_bk, new_bv = [], [], [], []
    for l in range(DEPTH):
        mp = _modulation(c_ctx, mod_w[l], mod_b[l])
        ms = _modulation(c, mod_w[l], mod_b[l])
        hp = _ffn_sublayer(hp, mp[0], mp[1], mp[2], norm_ffn1[l], ffn1_w13[l], ffn1_w2[l])
        hs = _ffn_sublayer(hs, ms[0], ms[1], ms[2], norm_ffn1[l], ffn1_w13[l], ffn1_w2[l])
        up = _modulate(_rms(hp, norm_mix[l]), mp[3], mp[4])
        op, ka, va, kb, vb = _mix_context(up, w_in[l], w_out[l], a_q_norm[l], a_k_norm[l], a_sink[l],
                                          b_q_norm[l], b_k_norm[l], conv_w[l], conv_b[l])
        hp = hp + mp[5] * op
        new_ak.append(ka)
        new_av.append(va)
        new_bk.append(kb)
        new_bv.append(vb)
        us = _modulate(_rms(hs, norm_mix[l]), ms[3], ms[4])
        os_ = _mix_latent(us, cache_a_k[:, l], cache_a_v[:, l], cache_b_k[:, l], cache_b_v[:, l], cos, sin,
                          w_in[l], w_out[l], a_q_norm[l], a_k_norm[l], a_sink[l],
                          b_q_norm[l], b_k_norm[l], b_rpb[l], conv_w[l], conv_b[l])
        hs = hs + ms[5] * os_
        hp = _ffn_sublayer(hp, mp[6], mp[7], mp[8], norm_ffn2[l], ffn2_w13[l], ffn2_w2[l])
        hs = _ffn_sublayer(hs, ms[6], ms[7], ms[8], norm_ffn2[l], ffn2_w13[l], ffn2_w2[l])
    return (hp, hs, jnp.stack(new_ak, axis=1), jnp.stack(new_av, axis=1),
            jnp.stack(new_bk, axis=1), jnp.stack(new_bv, axis=1))
```

```python
import functools

import jax
import jax.numpy as jnp
import numpy as np
from jax import lax
from jax.experimental import pallas as pl
from jax.experimental.pallas import tpu as pltpu

D = 4096
BATCH = 16
SEQ = 256
DEPTH = 2
DEC_BATCH = 2
DEC_SEQ = 4096
PAST = 512
GRID_W = 64
GRID_H = DEC_SEQ // GRID_W
HD = 128
N_Q_A = 12
N_KV_A = 4
GQA = N_Q_A // N_KV_A
WINDOW = 128
N_H_B = 12
NA_ROWS = 8
NA_COLS = 16
CONV_CH = 1024
D_FF = 11008
N_MOD = 9
ROPE_BASE = 10000.0
EPS = 1e-6
NEG_INF = -1e30
SCALE = HD ** -0.5

TP = BATCH * SEQ
TS = DEC_BATCH * DEC_SEQ
T = TP + TS
N_GROUPS = 1 + DEC_BATCH
W_AQ = N_Q_A * HD
W_AKV = N_KV_A * HD
W_B = N_H_B * HD
QKV_W = W_AQ + 2 * W_AKV + 3 * W_B
IN_W = QKV_W + 3 * CONV_CH
MIX_W = W_AQ + W_B + CONV_CH

FF_BLK = 256
N_FF_BLK = D_FF // FF_BLK
FF_PAD = (N_FF_BLK + 1) * FF_BLK

VMEM_LIMIT = 56 * 1024 * 1024

BF = jnp.bfloat16
F32 = jnp.float32


def _cparams(sem, vmem=VMEM_LIMIT):
    return pltpu.CompilerParams(dimension_semantics=sem, vmem_limit_bytes=vmem)


def _dot(a, b):
    return jnp.dot(a, b, preferred_element_type=F32)


def _dot_nt(a, b):
    return lax.dot_general(a, b, (((1,), (1,)), ((), ())), preferred_element_type=F32)


MOD_TN = 512
MOD_ROWS = 16


def _mod_kernel(c_ref, w_ref, b_ref, o_ref):
    c = c_ref[...]
    x = (c * jax.nn.sigmoid(c)).astype(BF)
    o_ref[...] = _dot(x, w_ref[...].astype(BF)) + b_ref[...]


def _modulation(cvec, mod_w, mod_b):
    n = N_MOD * D
    return pl.pallas_call(
        _mod_kernel,
        out_shape=jax.ShapeDtypeStruct((DEPTH, MOD_ROWS, n), F32),
        grid=(DEPTH, n // MOD_TN),
        in_specs=[
            pl.BlockSpec((MOD_ROWS, D), lambda l, j: (0, 0)),
            pl.BlockSpec((None, D, MOD_TN), lambda l, j: (l, 0, j)),
            pl.BlockSpec((None, 1, MOD_TN), lambda l, j: (l, 0, j)),
        ],
        out_specs=pl.BlockSpec((None, MOD_ROWS, MOD_TN), lambda l, j: (l, 0, j)),
        compiler_params=_cparams(("parallel", "parallel")),
        name="modulation",
    )(cvec, mod_w, mod_b.reshape(DEPTH, 1, n))


NORM_TM = 256


def _norm_kernel(h_ref, g_ref, shift_ref, scale_ref, o_ref):
    x = h_ref[...]
    ms = jnp.mean(x * x, axis=-1, keepdims=True)
    y = x * lax.rsqrt(ms + EPS) * g_ref[...]
    o_ref[...] = (y * (1.0 + scale_ref[...]) + shift_ref[...]).astype(BF)


def _mod_index(layer, k, rows_per_block):
    blocks_per_group = DEC_SEQ // rows_per_block

    def index(i):
        return layer * N_GROUPS * N_MOD + (i // blocks_per_group) * N_MOD + k
    return index


def _norm_mod(h, g, mods, layer, k_shift):
    ish = _mod_index(layer, k_shift, NORM_TM)
    isc = _mod_index(layer, k_shift + 1, NORM_TM)
    return pl.pallas_call(
        _norm_kernel,
        out_shape=jax.ShapeDtypeStruct((T, D), BF),
        grid=(T // NORM_TM,),
        in_specs=[
            pl.BlockSpec((NORM_TM, D), lambda i: (i, 0)),
            pl.BlockSpec((1, D), lambda i: (0, 0)),
            pl.BlockSpec((None, 1, D), lambda i: (ish(i), 0, 0)),
            pl.BlockSpec((None, 1, D), lambda i: (isc(i), 0, 0)),
        ],
        out_specs=pl.BlockSpec((NORM_TM, D), lambda i: (i, 0)),
        compiler_params=_cparams(("parallel",)),
        name="norm_mod",
    )(h, g, mods, mods)


UP_TM = 1024


def _ffn_up_kernel(x_ref, w1_ref, w3_ref, o_ref):
    j = pl.program_id(1)

    @pl.when(j < N_FF_BLK)
    def _():
        x = x_ref[...]
        a = _dot(x, w1_ref[...].astype(BF))
        b = _dot(x, w3_ref[...].astype(BF))
        o_ref[...] = (a * jax.nn.sigmoid(a) * b).astype(BF)

    @pl.when(j == N_FF_BLK)
    def _():
        o_ref[...] = jnp.zeros_like(o_ref)


def _ffn_up(xn, w13, layer):
    last = N_FF_BLK - 1
    return pl.pallas_call(
        _ffn_up_kernel,
        out_shape=jax.ShapeDtypeStruct((T, FF_PAD), BF),
        grid=(T // UP_TM, N_FF_BLK + 1),
        in_specs=[
            pl.BlockSpec((UP_TM, D), lambda i, j: (i, 0)),
            pl.BlockSpec((None, D, FF_BLK), lambda i, j: (layer, 0, jnp.minimum(j, last))),
            pl.BlockSpec((None, D, FF_BLK), lambda i, j: (layer, 0, N_FF_BLK + jnp.minimum(j, last))),
        ],
        out_specs=pl.BlockSpec((UP_TM, FF_BLK), lambda i, j: (i, j)),
        compiler_params=_cparams(("parallel", "arbitrary")),
        name="ffn_up",
    )(xn, w13, w13)


RES_TM = 1024
RES_TN = 1024


def _res_matmul_kernel(x_ref, w_ref, h_ref, gate_ref, o_ref, *, n_k, gate_mult):
    k = pl.program_id(2)
    part = _dot(x_ref[...], w_ref[...].astype(BF))

    @pl.when(k == 0)
    def _():
        o_ref[...] = part

    @pl.when(k > 0)
    def _():
        o_ref[...] += part

    @pl.when(k == n_k - 1)
    def _():
        o_ref[...] = h_ref[...] + (gate_mult * gate_ref[...]) * o_ref[...]


def _res_matmul(x, w, h, mods, layer, k_gate, gate_mult, tk):
    kdim = x.shape[1]
    n_k = kdim // tk
    ig = _mod_index(layer, k_gate, RES_TM)
    return pl.pallas_call(
        functools.partial(_res_matmul_kernel, n_k=n_k, gate_mult=gate_mult),
        out_shape=jax.ShapeDtypeStruct((T, D), F32),
        grid=(T // RES_TM, D // RES_TN, n_k),
        in_specs=[
            pl.BlockSpec((RES_TM, tk), lambda i, j, k: (i, k)),
            pl.BlockSpec((None, tk, RES_TN), lambda i, j, k: (layer, k, j)),
            pl.BlockSpec((RES_TM, RES_TN), lambda i, j, k: (i, j)),
            pl.BlockSpec((None, 1, RES_TN), lambda i, j, k: (ig(i), 0, j)),
        ],
        out_specs=pl.BlockSpec((RES_TM, RES_TN), lambda i, j, k: (i, j)),
        compiler_params=_cparams(("parallel", "parallel", "arbitrary")),
        name="res_matmul",
    )(x, w, h, mods)


IN_TM = 1024
IN_TN = 512


def _in_proj_kernel(x_ref, w_ref, o_ref):
    o_ref[...] = _dot(x_ref[...], w_ref[...].astype(BF))


def _in_proj(u, w_in, layer):
    return pl.pallas_call(
        _in_proj_kernel,
        out_shape=jax.ShapeDtypeStruct((T, IN_W), F32),
        grid=(T // IN_TM, IN_W // IN_TN),
        in_specs=[
            pl.BlockSpec((IN_TM, D), lambda i, j: (i, 0)),
            pl.BlockSpec((None, D, IN_TN), lambda i, j: (layer, 0, j)),
        ],
        out_specs=pl.BlockSpec((IN_TM, IN_TN), lambda i, j: (i, j)),
        compiler_params=_cparams(("parallel", "arbitrary")),
        name="in_proj",
    )(u, w_in)


PREP_TM = 256
C_QA = 0
C_KA = C_QA + W_AQ
C_VA = C_KA + W_AKV
C_QB = C_VA + W_AKV
C_KB = C_QB + W_B
C_VB = C_KB + W_B


def _head_rms(x, g):
    ms = jnp.mean(x * x, axis=-1, keepdims=True)
    return x * lax.rsqrt(ms + EPS) * g


def _rope(x, cos, sin_signed, lo_lane):
    swapped = jnp.where(lo_lane, pltpu.roll(x, HD - 32, axis=1), pltpu.roll(x, 32, axis=1))
    return x * cos + swapped * sin_signed


def _prep_body(p_ref, nrm_ref, qa_ref, ka_ref, va_ref, qb_ref, kb_ref, vb_ref, rope, caches):
    g_aq = nrm_ref[0:1, :]
    g_ak = nrm_ref[1:2, :]
    g_bq = nrm_ref[2:3, :]
    g_bk = nrm_ref[3:4, :]
    if rope is not None:
        cos, sin_signed = rope
        lo_lane = (lax.broadcasted_iota(jnp.int32, (PREP_TM, HD), 1) % 64) < 32

    def col(c0, h):
        return p_ref[:, c0 + h * HD: c0 + (h + 1) * HD]

    for h in range(N_Q_A):
        y = _head_rms(col(C_QA, h), g_aq)
        if rope is not None:
            y = _rope(y, cos, sin_signed, lo_lane)
        qa_ref[:, h * HD:(h + 1) * HD] = y.astype(BF)
    for h in range(N_KV_A):
        y = _head_rms(col(C_KA, h), g_ak)
        if caches is not None:
            caches[0][:, h * HD:(h + 1) * HD] = y
        if rope is not None:
            y = _rope(y, cos, sin_signed, lo_lane)
        ka_ref[:, h * HD:(h + 1) * HD] = y.astype(BF)
    va = p_ref[:, C_VA:C_VA + W_AKV]
    va_ref[...] = va.astype(BF)
    if caches is not None:
        caches[1][...] = va
    for h in range(N_H_B):
        qb_ref[:, h * HD:(h + 1) * HD] = _head_rms(col(C_QB, h), g_bq).astype(BF)
        y = _head_rms(col(C_KB, h), g_bk)
        if caches is not None:
            caches[2][:, h * HD:(h + 1) * HD] = y
        kb_ref[:, h * HD:(h + 1) * HD] = y.astype(BF)
    vb = p_ref[:, C_VB:C_VB + W_B]
    vb_ref[...] = vb.astype(BF)
    if caches is not None:
        caches[3][...] = vb


def _prep_ctx_kernel(p_ref, nrm_ref, qa_ref, ka_ref, va_ref, qb_ref, kb_ref, vb_ref,
                     cak_ref, cav_ref, cbk_ref, cbv_ref):
    _prep_body(p_ref, nrm_ref, qa_ref, ka_ref, va_ref, qb_ref, kb_ref, vb_ref,
               None, (cak_ref, cav_ref, cbk_ref, cbv_ref))


def _prep_lat_kernel(p_ref, nrm_ref, cos_ref, sin_ref, qa_ref, ka_ref, va_ref, qb_ref, kb_ref, vb_ref):
    _prep_body(p_ref, nrm_ref, qa_ref, ka_ref, va_ref, qb_ref, kb_ref, vb_ref,
               (cos_ref[...], sin_ref[...]), None)


def _qkv_specs(rows):
    widths = (W_AQ, W_AKV, W_AKV, W_B, W_B, W_B)
    shapes = [jax.ShapeDtypeStruct((rows, w), BF) for w in widths]
    specs = [pl.BlockSpec((PREP_TM, w), lambda i: (i, 0)) for w in widths]
    return shapes, specs


def _prep_ctx(p, nrm):
    shapes, specs = _qkv_specs(TP)
    cache_w = (W_AKV, W_AKV, W_B, W_B)
    shapes += [jax.ShapeDtypeStruct((TP, w), F32) for w in cache_w]
    specs += [pl.BlockSpec((PREP_TM, w), lambda i: (i, 0)) for w in cache_w]
    return pl.pallas_call(
        _prep_ctx_kernel,
        out_shape=shapes,
        grid=(TP // PREP_TM,),
        in_specs=[
            pl.BlockSpec((PREP_TM, QKV_W), lambda i: (i, 0)),
            pl.BlockSpec((8, HD), lambda i: (0, 0)),
        ],
        out_specs=specs,
        compiler_params=_cparams(("parallel",)),
        name="prep_ctx",
    )(p, nrm)


def _prep_lat(p, nrm, cos, sin_signed):
    shapes, specs = _qkv_specs(TS)
    off = TP // PREP_TM
    per_seq = DEC_SEQ // PREP_TM
    return pl.pallas_call(
        _prep_lat_kernel,
        out_shape=shapes,
        grid=(TS // PREP_TM,),
        in_specs=[
            pl.BlockSpec((PREP_TM, QKV_W), lambda i: (i + off, 0)),
            pl.BlockSpec((8, HD), lambda i: (0, 0)),
            pl.BlockSpec((PREP_TM, HD), lambda i: (i % per_seq, 0)),
            pl.BlockSpec((PREP_TM, HD), lambda i: (i % per_seq, 0)),
        ],
        out_specs=specs,
        compiler_params=_cparams(("parallel",)),
        name="prep_lat",
    )(p, nrm, cos, sin_signed)


def _conv_kernel(xc_ref, gb_ref, gc_ref, w_ref, b_ref, o_ref):
    u = gc_ref[...] * xc_ref[...]
    n = u.shape[0]
    row = lax.broadcasted_iota(jnp.int32, u.shape, 0)
    prev = jnp.where(row == 0, 0.0, pltpu.roll(u, 1, axis=0))
    nxt = jnp.where(row == n - 1, 0.0, pltpu.roll(u, n - 1, axis=0))
    y = w_ref[0:1, :] * prev + w_ref[1:2, :] * u + w_ref[2:3, :] * nxt + b_ref[...]
    o_ref[...] = (gb_ref[...] * y).astype(BF)


def _short_conv(p, conv_w, conv_b, rows, row_off, n_seq, ch):
    c0 = QKV_W // ch
    per = CONV_CH // ch
    return pl.pallas_call(
        _conv_kernel,
        out_shape=jax.ShapeDtypeStruct((n_seq * rows, CONV_CH), BF),
        grid=(n_seq, per),
        in_specs=[
            pl.BlockSpec((rows, ch), lambda s, c: (s + row_off, c0 + c)),
            pl.BlockSpec((rows, ch), lambda s, c: (s + row_off, c0 + per + c)),
            pl.BlockSpec((rows, ch), lambda s, c: (s + row_off, c0 + 2 * per + c)),
            pl.BlockSpec((8, ch), lambda s, c: (0, c)),
            pl.BlockSpec((1, ch), lambda s, c: (0, c)),
        ],
        out_specs=pl.BlockSpec((rows, ch), lambda s, c: (s, c)),
        compiler_params=_cparams(("parallel", "parallel")),
        name="short_conv",
    )(p, p, p, conv_w, conv_b)


def _softmax_parts(parts, sink):
    m = parts[0].max(axis=-1, keepdims=True)
    for s in parts[1:]:
        m = jnp.maximum(m, s.max(axis=-1, keepdims=True))
    if sink is not None:
        m = jnp.maximum(m, sink)
    es = [jnp.exp(s - m) for s in parts]
    den = es[0].sum(axis=-1, keepdims=True)
    for e in es[1:]:
        den = den + e.sum(axis=-1, keepdims=True)
    if sink is not None:
        den = den + jnp.exp(sink - m)
    inv = 1.0 / den
    return [(e * inv).astype(BF) for e in es]


def _sink_column(sink_ref, kv, rows):
    return jnp.concatenate(
        [jnp.full((rows, 1), sink_ref[kv * GQA + g], F32) for g in range(GQA)], axis=0)


def _attn_ctx_kernel(sink_ref, qa_ref, ka_ref, va_ref, qb_ref, kb_ref, vb_ref, o_ref):
    for kv in range(N_KV_A):
        k = ka_ref[:, kv * HD:(kv + 1) * HD]
        v = va_ref[:, kv * HD:(kv + 1) * HD]
        q = jnp.concatenate(
            [qa_ref[:, (kv * GQA + g) * HD:(kv * GQA + g + 1) * HD] for g in range(GQA)], axis=0)
        s = _dot_nt(q, k) * SCALE
        (p,) = _softmax_parts([s], _sink_column(sink_ref, kv, SEQ))
        o = _dot(p, v)
        for g in range(GQA):
            h = kv * GQA + g
            o_ref[:, h * HD:(h + 1) * HD] = o[g * SEQ:(g + 1) * SEQ].astype(BF)
    for h in range(N_H_B):
        sl = slice(h * HD, (h + 1) * HD)
        s = _dot_nt(qb_ref[:, sl], kb_ref[:, sl]) * SCALE
        (p,) = _softmax_parts([s], None)
        o_ref[:, W_AQ + h * HD:W_AQ + (h + 1) * HD] = _dot(p, vb_ref[:, sl]).astype(BF)


def _attn_ctx(sink, qa, ka, va, qb, kb, vb):
    def spec(w):
        return pl.BlockSpec((SEQ, w), lambda b: (b, 0))
    return pl.pallas_call(
        _attn_ctx_kernel,
        out_shape=jax.ShapeDtypeStruct((TP, W_AQ + W_B), BF),
        grid=(BATCH,),
        in_specs=[pl.BlockSpec(memory_space=pltpu.SMEM),
                  spec(W_AQ), spec(W_AKV), spec(W_AKV), spec(W_B), spec(W_B), spec(W_B)],
        out_specs=spec(W_AQ + W_B),
        compiler_params=_cparams(("parallel",)),
        name="attn_ctx",
    )(sink, qa, ka, va, qb, kb, vb)


QB = 128
SPAN = QB + 2 * WINDOW


def _attn_win_kernel(sink_ref, q_ref, k_ref, v_ref, ck_ref, cv_ref, o_ref):
    i = pl.program_id(1)
    start = i * QB
    s0 = pl.multiple_of(jnp.clip(start - WINDOW, 0, DEC_SEQ - SPAN), QB)
    qpos = start + lax.broadcasted_iota(jnp.int32, (GQA * QB, SPAN), 0) % QB
    kpos = s0 + lax.broadcasted_iota(jnp.int32, (GQA * QB, SPAN), 1)
    valid = jnp.abs(qpos - kpos) <= WINDOW
    for kv in range(N_KV_A):
        sl = slice(kv * HD, (kv + 1) * HD)
        k = k_ref[pl.ds(s0, SPAN), sl]
        v = v_ref[pl.ds(s0, SPAN), sl]
        ck = ck_ref[:, sl].astype(BF)
        cv = cv_ref[:, sl].astype(BF)
        q = jnp.concatenate(
            [q_ref[:, (kv * GQA + g) * HD:(kv * GQA + g + 1) * HD] for g in range(GQA)], axis=0)
        s_loc = jnp.where(valid, _dot_nt(q, k) * SCALE, NEG_INF)
        s_ctx = _dot_nt(q, ck) * SCALE
        p_loc, p_ctx = _softmax_parts([s_loc, s_ctx], _sink_column(sink_ref, kv, QB))
        o = _dot(p_loc, v) + _dot(p_ctx, cv)
        for g in range(GQA):
            h = kv * GQA + g
            o_ref[:, h * HD:(h + 1) * HD] = o[g * QB:(g + 1) * QB].astype(BF)


def _attn_win(sink, qa, ka, va, ck, cv):
    nb = DEC_SEQ // QB
    return pl.pallas_call(
        _attn_win_kernel,
        out_shape=jax.ShapeDtypeStruct((TS, W_AQ), BF),
        grid=(DEC_BATCH, nb),
        in_specs=[
            pl.BlockSpec(memory_space=pltpu.SMEM),
            pl.BlockSpec((QB, W_AQ), lambda b, i: (b * nb + i, 0)),
            pl.BlockSpec((DEC_SEQ, W_AKV), lambda b, i: (b, 0)),
            pl.BlockSpec((DEC_SEQ, W_AKV), lambda b, i: (b, 0)),
            pl.BlockSpec((None, PAST, W_AKV), lambda b, i: (b, 0, 0)),
            pl.BlockSpec((None, PAST, W_AKV), lambda b, i: (b, 0, 0)),
        ],
        out_specs=pl.BlockSpec((QB, W_AQ), lambda b, i: (b * nb + i, 0)),
        compiler_params=_cparams(("parallel", "arbitrary")),
        name="attn_win",
    )(sink, qa, ka, va, ck, cv)


NB_HG = 4
NB_KEYS = NA_ROWS * GRID_W


def _row_start(r):
    return jnp.clip(r - NA_ROWS // 2, 0, GRID_H - NA_ROWS)


def _attn_nbr_kernel(q_ref, k_ref, v_ref, ck_ref, cv_ref, bias_ref, o_ref):
    r = pl.program_id(2)
    k0 = pl.multiple_of(_row_start(r) * GRID_W, GRID_W)
    for h in range(NB_HG):
        sl = slice(h * HD, (h + 1) * HD)
        q = q_ref[:, sl]
        k = k_ref[pl.ds(k0, NB_KEYS), sl]
        v = v_ref[pl.ds(k0, NB_KEYS), sl]
        s_loc = _dot_nt(q, k) * SCALE + bias_ref[h]
        s_ctx = _dot_nt(q, ck_ref[:, sl].astype(BF)) * SCALE
        p_loc, p_ctx = _softmax_parts([s_loc, s_ctx], None)
        o = _dot(p_loc, v) + _dot(p_ctx, cv_ref[:, sl].astype(BF))
        o_ref[:, sl] = o.astype(BF)


def _attn_nbr(qb, kb, vb, ck, cv, bias):
    hw = NB_HG * HD
    n_hg = N_H_B // NB_HG
    return pl.pallas_call(
        _attn_nbr_kernel,
        out_shape=jax.ShapeDtypeStruct((TS, W_B), BF),
        grid=(DEC_BATCH, n_hg, GRID_H),
        in_specs=[
            pl.BlockSpec((GRID_W, hw), lambda b, g, r: (b * GRID_H + r, g)),
            pl.BlockSpec((DEC_SEQ, hw), lambda b, g, r: (b, g)),
            pl.BlockSpec((DEC_SEQ, hw), lambda b, g, r: (b, g)),
            pl.BlockSpec((None, PAST, hw), lambda b, g, r: (b, 0, g)),
            pl.BlockSpec((None, PAST, hw), lambda b, g, r: (b, 0, g)),
            pl.BlockSpec((None, NB_HG, GRID_W, NB_KEYS), lambda b, g, r: (r - _row_start(r), g, 0, 0)),
        ],
        out_specs=pl.BlockSpec((GRID_W, hw), lambda b, g, r: (b * GRID_H + r, g)),
        compiler_params=_cparams(("parallel", "parallel", "arbitrary")),
        name="attn_nbr",
    )(qb, kb, vb, ck, cv, bias)


def _rope_tables():
    pairs = HD // 4
    t = np.arange(DEC_SEQ)
    pos = np.stack([t // GRID_W, t % GRID_W], axis=-1).astype(np.float32)
    inv = jnp.asarray(ROPE_BASE, F32) ** (-jnp.arange(pairs, dtype=F32) * 2.0 / (HD // 2))
    ang = jnp.asarray(pos)[:, :, None] * inv
    cos = jnp.cos(ang)
    sin = jnp.sin(ang)
    cos_l = jnp.concatenate([cos, cos], axis=-1).reshape(DEC_SEQ, HD)
    sin_l = jnp.concatenate([-sin, sin], axis=-1).reshape(DEC_SEQ, HD)
    return cos_l, sin_l


def _nbr_bias_table(rpb):
    qc = np.arange(GRID_W)
    cs = np.clip(qc - NA_COLS // 2, 0, GRID_W - NA_COLS)
    col_ok = (qc[None, :] >= cs[:, None]) & (qc[None, :] < cs[:, None] + NA_COLS)
    dc = np.clip(qc[None, :] - qc[:, None], -(NA_COLS - 1), NA_COLS - 1) + NA_COLS - 1
    s = np.arange(NA_ROWS)
    j = np.arange(NA_ROWS)
    dr = j[None, :] - s[:, None] + NA_ROWS - 1
    b = rpb.astype(F32)[:, dr[:, :, None, None], dc[None, None, :, :]]
    b = jnp.where(jnp.asarray(col_ok)[None, None, None], b, NEG_INF)
    b = jnp.transpose(b, (1, 0, 3, 2, 4))
    return b.reshape(NA_ROWS, N_H_B, GRID_W, NB_KEYS)


def kernel(x_prompt, x_sample, cache_a_k, cache_a_v, cache_b_k, cache_b_v, c, c_ctx,
           mod_w, mod_b, norm_ffn1, norm_mix, norm_ffn2, ffn1_w13, ffn1_w2, ffn2_w13, ffn2_w2,
           w_in, w_out, a_q_norm, a_k_norm, a_sink, b_q_norm, b_k_norm, b_rpb, conv_w, conv_b):
    h = jnp.concatenate([x_prompt.reshape(TP, D), x_sample.reshape(TS, D)], axis=0)

    cvec = jnp.concatenate([c_ctx[None, :], c, jnp.zeros((MOD_ROWS - N_GROUPS, D), F32)], axis=0)
    mods = _modulation(cvec, mod_w, mod_b)
    mods = mods[:, :N_GROUPS].reshape(DEPTH * N_GROUPS * N_MOD, 1, D)

    cos_l, sin_l = _rope_tables()
    pad_rows = ((0, 0), (0, FF_PAD - D_FF), (0, 0))
    ffn1_w2p = jnp.pad(ffn1_w2.astype(BF), pad_rows)
    ffn2_w2p = jnp.pad(ffn2_w2.astype(BF), pad_rows)

    new_ak, new_av, new_bk, new_bv = [], [], [], []
    for l in range(DEPTH):
        def ffn(h, g, w13, w2, k0):
            xn = _norm_mod(h, g[l][None, :], mods, l, k0)
            hid = _ffn_up(xn, w13, l)
            return _res_matmul(hid, w2, h, mods, l, k0 + 2, 0.5, FF_PAD // 4)

        h = ffn(h, norm_ffn1, ffn1_w13, ffn1_w2p, 0)

        u = _norm_mod(h, norm_mix[l][None, :], mods, l, 3)
        p = _in_proj(u, w_in, l)
        nrm = jnp.concatenate([a_q_norm[l][None], a_k_norm[l][None], b_q_norm[l][None], b_k_norm[l][None],
                               jnp.zeros((4, HD), F32)], axis=0)
        qa_c, ka_c, va_c, qb_c, kb_c, vb_c, cak, cav, cbk, cbv = _prep_ctx(p, nrm)
        qa_s, ka_s, va_s, qb_s, kb_s, vb_s = _prep_lat(p, nrm, cos_l, sin_l)
        new_ak.append(cak.reshape(BATCH, SEQ, N_KV_A, HD))
        new_av.append(cav.reshape(BATCH, SEQ, N_KV_A, HD))
        new_bk.append(cbk.reshape(BATCH, SEQ, N_H_B, HD))
        new_bv.append(cbv.reshape(BATCH, SEQ, N_H_B, HD))

        cw = jnp.concatenate([conv_w[l], jnp.zeros((5, CONV_CH), F32)], axis=0)
        cb = conv_b[l][None, :]
        oc_c = _short_conv(p, cw, cb, SEQ, 0, BATCH, CONV_CH)
        oc_s = _short_conv(p, cw, cb, DEC_SEQ, TP // DEC_SEQ, DEC_BATCH, 128)

        sink = a_sink[l]
        o_c = _attn_ctx(sink, qa_c, ka_c, va_c, qb_c, kb_c, vb_c)
        oa_s = _attn_win(sink, qa_s, ka_s, va_s,
                         cache_a_k[:, l].reshape(DEC_BATCH, PAST, W_AKV),
                         cache_a_v[:, l].reshape(DEC_BATCH, PAST, W_AKV))
        ob_s = _attn_nbr(qb_s, kb_s, vb_s,
                         cache_b_k[:, l].reshape(DEC_BATCH, PAST, W_B),
                         cache_b_v[:, l].reshape(DEC_BATCH, PAST, W_B),
                         _nbr_bias_table(b_rpb[l]))
        mix = jnp.concatenate([
            jnp.concatenate([o_c, oc_c], axis=1),
            jnp.concatenate([oa_s, ob_s, oc_s], axis=1)], axis=0)
        h = _res_matmul(mix, w_out, h, mods, l, 5, 1.0, MIX_W // 2)

        h = ffn(h, norm_ffn2, ffn2_w13, ffn2_w2p, 6)

    return (h[:TP].reshape(BATCH, SEQ, D), h[TP:].reshape(DEC_BATCH, DEC_SEQ, D),
            jnp.stack(new_ak, axis=1), jnp.stack(new_av, axis=1),
            jnp.stack(new_bk, axis=1), jnp.stack(new_bv, axis=1))
```

```python
import functools

import jax
import jax.numpy as jnp
import numpy as np
from jax import lax
from jax.experimental import pallas as pl
from jax.experimental.pallas import tpu as pltpu

D = 4096
BATCH = 16
SEQ = 256
DEPTH = 2
DEC_BATCH = 2
DEC_SEQ = 4096
PAST = 512
GRID_W = 64
GRID_H = DEC_SEQ // GRID_W
HD = 128
N_Q_A = 12
N_KV_A = 4
GQA = N_Q_A // N_KV_A
WINDOW = 128
N_H_B = 12
NA_ROWS = 8
NA_COLS = 16
CONV_CH = 1024
D_FF = 11008
N_MOD = 9
ROPE_BASE = 10000.0
EPS = 1e-6
NEG_INF = -1e30
SCALE = HD ** -0.5

TP = BATCH * SEQ
TS = DEC_BATCH * DEC_SEQ
N_GROUPS = 1 + DEC_BATCH
W_AQ = N_Q_A * HD
W_AKV = N_KV_A * HD
W_B = N_H_B * HD
QKV_W = W_AQ + 2 * W_AKV + 3 * W_B
IN_W = QKV_W + 3 * CONV_CH
MIX_W = W_AQ + W_B + CONV_CH

FF_BLK = 256
N_FF_BLK = D_FF // FF_BLK
FF_PAD = (N_FF_BLK + 1) * FF_BLK

VMEM_LIMIT = 56 * 1024 * 1024

BF = jnp.bfloat16
F32 = jnp.float32


def _cparams(sem, vmem=VMEM_LIMIT):
    return pltpu.CompilerParams(dimension_semantics=sem, vmem_limit_bytes=vmem)


def _dot(a, b):
    return jnp.dot(a, b, preferred_element_type=F32)


def _dot_nt(a, b):
    return lax.dot_general(a, b, (((1,), (1,)), ((), ())), preferred_element_type=F32)


def _mod_index(layer, k, rows_per_block, latent):
    base = layer * N_GROUPS * N_MOD + k
    if not latent:
        return lambda i: base
    blocks_per_group = DEC_SEQ // rows_per_block
    return lambda i: base + (1 + i // blocks_per_group) * N_MOD


MOD_TN = 512
MOD_ROWS = 16


def _mod_kernel(c_ref, w_ref, b_ref, o_ref):
    c = c_ref[...]
    x = (c * jax.nn.sigmoid(c)).astype(BF)
    o_ref[...] = _dot(x, w_ref[...].astype(BF)) + b_ref[...]


def _modulation(cvec, mod_w, mod_b):
    n = N_MOD * D
    return pl.pallas_call(
        _mod_kernel,
        out_shape=jax.ShapeDtypeStruct((DEPTH, MOD_ROWS, n), F32),
        grid=(DEPTH, n // MOD_TN),
        in_specs=[
            pl.BlockSpec((MOD_ROWS, D), lambda l, j: (0, 0)),
            pl.BlockSpec((None, D, MOD_TN), lambda l, j: (l, 0, j)),
            pl.BlockSpec((None, 1, MOD_TN), lambda l, j: (l, 0, j)),
        ],
        out_specs=pl.BlockSpec((None, MOD_ROWS, MOD_TN), lambda l, j: (l, 0, j)),
        compiler_params=_cparams(("parallel", "parallel")),
        name="modulation",
    )(cvec, mod_w, mod_b.reshape(DEPTH, 1, n))


NORM_TM = 512


def _norm_kernel(h_ref, g_ref, shift_ref, scale_ref, o_ref):
    x = h_ref[...]
    ms = jnp.mean(x * x, axis=-1, keepdims=True)
    y = x * lax.rsqrt(ms + EPS) * g_ref[...]
    o_ref[...] = (y * (1.0 + scale_ref[...]) + shift_ref[...]).astype(BF)


def _norm_mod(h, g, mods, layer, k_shift, latent):
    rows = h.shape[0]
    ish = _mod_index(layer, k_shift, NORM_TM, latent)
    isc = _mod_index(layer, k_shift + 1, NORM_TM, latent)
    return pl.pallas_call(
        _norm_kernel,
        out_shape=jax.ShapeDtypeStruct((rows, D), BF),
        grid=(rows // NORM_TM,),
        in_specs=[
            pl.BlockSpec((NORM_TM, D), lambda i: (i, 0)),
            pl.BlockSpec((1, D), lambda i: (0, 0)),
            pl.BlockSpec((None, 1, D), lambda i: (ish(i), 0, 0)),
            pl.BlockSpec((None, 1, D), lambda i: (isc(i), 0, 0)),
        ],
        out_specs=pl.BlockSpec((NORM_TM, D), lambda i: (i, 0)),
        compiler_params=_cparams(("parallel",)),
        name="norm_mod",
    )(h, g, mods, mods)


MM_TM = 1024


def _ffn_up_kernel(x_ref, w1_ref, w3_ref, o_ref):
    j = pl.program_id(1)

    @pl.when(j < N_FF_BLK)
    def _():
        x = x_ref[...]
        a = _dot(x, w1_ref[...].astype(BF))
        b = _dot(x, w3_ref[...].astype(BF))
        o_ref[...] = (a * jax.nn.sigmoid(a) * b).astype(BF)

    @pl.when(j == N_FF_BLK)
    def _():
        o_ref[...] = jnp.zeros_like(o_ref)


def _ffn_up(xn, w13, layer):
    rows = xn.shape[0]
    last = N_FF_BLK - 1
    return pl.pallas_call(
        _ffn_up_kernel,
        out_shape=jax.ShapeDtypeStruct((rows, FF_PAD), BF),
        grid=(rows // MM_TM, N_FF_BLK + 1),
        in_specs=[
            pl.BlockSpec((MM_TM, D), lambda i, j: (i, 0)),
            pl.BlockSpec((None, D, FF_BLK), lambda i, j: (layer, 0, jnp.minimum(j, last))),
            pl.BlockSpec((None, D, FF_BLK), lambda i, j: (layer, 0, N_FF_BLK + jnp.minimum(j, last))),
        ],
        out_specs=pl.BlockSpec((MM_TM, FF_BLK), lambda i, j: (i, j)),
        compiler_params=_cparams(("parallel", "arbitrary")),
        name="ffn_up",
    )(xn, w13, w13)


DOWN_TN = 1024
DOWN_TK = FF_PAD // 4


def _ffn_down_kernel(x_ref, w_ref, h_ref, gate_ref, o_ref):
    k = pl.program_id(2)
    part = _dot(x_ref[...], w_ref[...])

    @pl.when(k == 0)
    def _():
        o_ref[...] = part

    @pl.when(k > 0)
    def _():
        o_ref[...] += part

    @pl.when(k == pl.num_programs(2) - 1)
    def _():
        o_ref[...] = h_ref[...] + (0.5 * gate_ref[...]) * o_ref[...]


def _ffn_down(x, w2p, h, mods, layer, k_gate, latent):
    rows = x.shape[0]
    ig = _mod_index(layer, k_gate, MM_TM, latent)
    return pl.pallas_call(
        _ffn_down_kernel,
        out_shape=jax.ShapeDtypeStruct((rows, D), F32),
        grid=(rows // MM_TM, D // DOWN_TN, FF_PAD // DOWN_TK),
        in_specs=[
            pl.BlockSpec((MM_TM, DOWN_TK), lambda i, j, k: (i, k)),
            pl.BlockSpec((None, DOWN_TK, DOWN_TN), lambda i, j, k: (layer, k, j)),
            pl.BlockSpec((MM_TM, DOWN_TN), lambda i, j, k: (i, j)),
            pl.BlockSpec((None, 1, DOWN_TN), lambda i, j, k: (ig(i), 0, j)),
        ],
        out_specs=pl.BlockSpec((MM_TM, DOWN_TN), lambda i, j, k: (i, j)),
        compiler_params=_cparams(("parallel", "parallel", "arbitrary")),
        name="ffn_down",
    )(x, w2p, h, mods)


IN_TN = 512


def _in_proj_kernel(x_ref, w_ref, o_ref):
    o_ref[...] = _dot(x_ref[...], w_ref[...].astype(BF))


def _in_proj(u, w_in, layer):
    rows = u.shape[0]
    return pl.pallas_call(
        _in_proj_kernel,
        out_shape=jax.ShapeDtypeStruct((rows, IN_W), F32),
        grid=(rows // MM_TM, IN_W // IN_TN),
        in_specs=[
            pl.BlockSpec((MM_TM, D), lambda i, j: (i, 0)),
            pl.BlockSpec((None, D, IN_TN), lambda i, j: (layer, 0, j)),
        ],
        out_specs=pl.BlockSpec((MM_TM, IN_TN), lambda i, j: (i, j)),
        compiler_params=_cparams(("parallel", "arbitrary")),
        name="in_proj",
    )(u, w_in)


OUT_TN = 512


def _out_proj_kernel(*refs, n_lhs):
    xs, ws = refs[:n_lhs], refs[n_lhs:2 * n_lhs]
    h_ref, gate_ref, o_ref = refs[2 * n_lhs:]
    acc = _dot(xs[0][...], ws[0][...].astype(BF))
    for x_ref, w_ref in zip(xs[1:], ws[1:]):
        acc += _dot(x_ref[...], w_ref[...].astype(BF))
    o_ref[...] = h_ref[...] + gate_ref[...] * acc


def _out_proj(xs, w_out, h, mods, layer, latent):
    rows = h.shape[0]
    ig = _mod_index(layer, 5, MM_TM, latent)
    x_specs, w_specs, c0 = [], [], 0
    for x in xs:
        w = x.shape[1]
        assert c0 % w == 0
        x_specs.append(pl.BlockSpec((MM_TM, w), lambda i, j: (i, 0)))
        w_specs.append(pl.BlockSpec((None, w, OUT_TN), functools.partial(lambda i, j, r: (layer, r, j), r=c0 // w)))
        c0 += w
    assert c0 == MIX_W
    return pl.pallas_call(
        functools.partial(_out_proj_kernel, n_lhs=len(xs)),
        out_shape=jax.ShapeDtypeStruct((rows, D), F32),
        grid=(rows // MM_TM, D // OUT_TN),
        in_specs=x_specs + w_specs + [
            pl.BlockSpec((MM_TM, OUT_TN), lambda i, j: (i, j)),
            pl.BlockSpec((None, 1, OUT_TN), lambda i, j: (ig(i), 0, j)),
        ],
        out_specs=pl.BlockSpec((MM_TM, OUT_TN), lambda i, j: (i, j)),
        compiler_params=_cparams(("parallel", "arbitrary")),
        name="out_proj",
    )(*xs, *([w_out] * len(xs)), h, mods)


PREP_TM = 256
C_QA = 0
C_KA = C_QA + W_AQ
C_VA = C_KA + W_AKV
C_QB = C_VA + W_AKV
C_KB = C_QB + W_B
C_VB = C_KB + W_B


def _head_rms(x, g):
    ms = jnp.mean(x * x, axis=-1, keepdims=True)
    return x * lax.rsqrt(ms + EPS) * g


def _rope(x, cos, sin_signed, lo_lane):
    swapped = jnp.where(lo_lane, pltpu.roll(x, HD - 32, axis=1), pltpu.roll(x, 32, axis=1))
    return x * cos + swapped * sin_signed


def _prep_body(p_ref, nrm_ref, qa_ref, ka_ref, va_ref, qb_ref, kb_ref, vb_ref, rope, caches):
    g_aq = nrm_ref[0:1, :]
    g_ak = nrm_ref[1:2, :]
    g_bq = nrm_ref[2:3, :]
    g_bk = nrm_ref[3:4, :]
    if rope is not None:
        cos, sin_signed = rope
        lo_lane = (lax.broadcasted_iota(jnp.int32, (PREP_TM, HD), 1) % 64) < 32

    def col(c0, h):
        return p_ref[:, c0 + h * HD: c0 + (h + 1) * HD]

    for h in range(N_Q_A):
        y = _head_rms(col(C_QA, h), g_aq)
        if rope is not None:
            y = _rope(y, cos, sin_signed, lo_lane)
        qa_ref[:, h * HD:(h + 1) * HD] = y.astype(BF)
    for h in range(N_KV_A):
        y = _head_rms(col(C_KA, h), g_ak)
        if caches is not None:
            caches[0][:, h * HD:(h + 1) * HD] = y
        if rope is not None:
            y = _rope(y, cos, sin_signed, lo_lane)
        ka_ref[:, h * HD:(h + 1) * HD] = y.astype(BF)
    va = p_ref[:, C_VA:C_VA + W_AKV]
    va_ref[...] = va.astype(BF)
    if caches is not None:
        caches[1][...] = va
    for h in range(N_H_B):
        qb_ref[:, h * HD:(h + 1) * HD] = _head_rms(col(C_QB, h), g_bq).astype(BF)
        y = _head_rms(col(C_KB, h), g_bk)
        if caches is not None:
            caches[2][:, h * HD:(h + 1) * HD] = y
        kb_ref[:, h * HD:(h + 1) * HD] = y.astype(BF)
    vb = p_ref[:, C_VB:C_VB + W_B]
    vb_ref[...] = vb.astype(BF)
    if caches is not None:
        caches[3][...] = vb


def _prep_ctx_kernel(p_ref, nrm_ref, qa_ref, ka_ref, va_ref, qb_ref, kb_ref, vb_ref,
                     cak_ref, cav_ref, cbk_ref, cbv_ref):
    _prep_body(p_ref, nrm_ref, qa_ref, ka_ref, va_ref, qb_ref, kb_ref, vb_ref,
               None, (cak_ref, cav_ref, cbk_ref, cbv_ref))


def _prep_lat_kernel(p_ref, nrm_ref, cos_ref, sin_ref, qa_ref, ka_ref, va_ref, qb_ref, kb_ref, vb_ref):
    _prep_body(p_ref, nrm_ref, qa_ref, ka_ref, va_ref, qb_ref, kb_ref, vb_ref,
               (cos_ref[...], sin_ref[...]), None)


def _qkv_specs(rows):
    widths = (W_AQ, W_AKV, W_AKV, W_B, W_B, W_B)
    shapes = [jax.ShapeDtypeStruct((rows, w), BF) for w in widths]
    specs = [pl.BlockSpec((PREP_TM, w), lambda i: (i, 0)) for w in widths]
    return shapes, specs


def _prep_ctx(p, nrm):
    shapes, specs = _qkv_specs(TP)
    cache_w = (W_AKV, W_AKV, W_B, W_B)
    shapes += [jax.ShapeDtypeStruct((TP, w), F32) for w in cache_w]
    specs += [pl.BlockSpec((PREP_TM, w), lambda i: (i, 0)) for w in cache_w]
    return pl.pallas_call(
        _prep_ctx_kernel,
        out_shape=shapes,
        grid=(TP // PREP_TM,),
        in_specs=[
            pl.BlockSpec((PREP_TM, QKV_W), lambda i: (i, 0)),
            pl.BlockSpec((8, HD), lambda i: (0, 0)),
        ],
        out_specs=specs,
        compiler_params=_cparams(("parallel",)),
        name="prep_ctx",
    )(p, nrm)


def _prep_lat(p, nrm, cos, sin_signed):
    shapes, specs = _qkv_specs(TS)
    per_seq = DEC_SEQ // PREP_TM
    return pl.pallas_call(
        _prep_lat_kernel,
        out_shape=shapes,
        grid=(TS // PREP_TM,),
        in_specs=[
            pl.BlockSpec((PREP_TM, QKV_W), lambda i: (i, 0)),
            pl.BlockSpec((8, HD), lambda i: (0, 0)),
            pl.BlockSpec((PREP_TM, HD), lambda i: (i % per_seq, 0)),
            pl.BlockSpec((PREP_TM, HD), lambda i: (i % per_seq, 0)),
        ],
        out_specs=specs,
        compiler_params=_cparams(("parallel",)),
        name="prep_lat",
    )(p, nrm, cos, sin_signed)


def _conv_kernel(xc_ref, gb_ref, gc_ref, w_ref, b_ref, o_ref):
    u = gc_ref[...] * xc_ref[...]
    n = u.shape[0]
    row = lax.broadcasted_iota(jnp.int32, u.shape, 0)
    prev = jnp.where(row == 0, 0.0, pltpu.roll(u, 1, axis=0))
    nxt = jnp.where(row == n - 1, 0.0, pltpu.roll(u, n - 1, axis=0))
    y = w_ref[0:1, :] * prev + w_ref[1:2, :] * u + w_ref[2:3, :] * nxt + b_ref[...]
    o_ref[...] = (gb_ref[...] * y).astype(BF)


def _short_conv(p, conv_w, conv_b, rows, ch):
    n_seq = p.shape[0] // rows
    c0 = QKV_W // ch
    per = CONV_CH // ch
    return pl.pallas_call(
        _conv_kernel,
        out_shape=jax.ShapeDtypeStruct((n_seq * rows, CONV_CH), BF),
        grid=(n_seq, per),
        in_specs=[
            pl.BlockSpec((rows, ch), lambda s, c: (s, c0 + c)),
            pl.BlockSpec((rows, ch), lambda s, c: (s, c0 + per + c)),
            pl.BlockSpec((rows, ch), lambda s, c: (s, c0 + 2 * per + c)),
            pl.BlockSpec((8, ch), lambda s, c: (0, c)),
            pl.BlockSpec((1, ch), lambda s, c: (0, c)),
        ],
        out_specs=pl.BlockSpec((rows, ch), lambda s, c: (s, c)),
        compiler_params=_cparams(("parallel", "parallel")),
        name="short_conv",
    )(p, p, p, conv_w, conv_b)


def _softmax_parts(parts, sink):
    m = parts[0].max(axis=-1, keepdims=True)
    for s in parts[1:]:
        m = jnp.maximum(m, s.max(axis=-1, keepdims=True))
    if sink is not None:
        m = jnp.maximum(m, sink)
    es = [jnp.exp(s - m) for s in parts]
    den = es[0].sum(axis=-1, keepdims=True)
    for e in es[1:]:
        den = den + e.sum(axis=-1, keepdims=True)
    if sink is not None:
        den = den + jnp.exp(sink - m)
    inv = 1.0 / den
    return [(e * inv).astype(BF) for e in es]


def _sink_column(sink_ref, kv, rows):
    return jnp.concatenate(
        [jnp.full((rows, 1), sink_ref[kv * GQA + g], F32) for g in range(GQA)], axis=0)


def _group_rows(x, kv):
    return jnp.concatenate([x[:, (kv * GQA + g) * HD:(kv * GQA + g + 1) * HD] for g in range(GQA)], axis=0)


def _ungroup_rows(o, rows):
    return [o[g * rows:(g + 1) * rows] for g in range(GQA)]


def _attn_ctx_kernel(sink_ref, qa_ref, ka_ref, va_ref, qb_ref, kb_ref, vb_ref, o_ref):
    qa, ka, va = qa_ref[...], ka_ref[...], va_ref[...]
    qb, kb, vb = qb_ref[...], kb_ref[...], vb_ref[...]
    outs = []
    for kv in range(N_KV_A):
        sl = slice(kv * HD, (kv + 1) * HD)
        s = _dot_nt(_group_rows(qa, kv), ka[:, sl]) * SCALE
        (p,) = _softmax_parts([s], _sink_column(sink_ref, kv, SEQ))
        outs += _ungroup_rows(_dot(p, va[:, sl]), SEQ)
    for h in range(N_H_B):
        sl = slice(h * HD, (h + 1) * HD)
        s = _dot_nt(qb[:, sl], kb[:, sl]) * SCALE
        (p,) = _softmax_parts([s], None)
        outs.append(_dot(p, vb[:, sl]))
    o_ref[...] = jnp.concatenate(outs, axis=1).astype(BF)


def _attn_ctx(sink, qa, ka, va, qb, kb, vb):
    def spec(w):
        return pl.BlockSpec((SEQ, w), lambda b: (b, 0))
    return pl.pallas_call(
        _attn_ctx_kernel,
        out_shape=jax.ShapeDtypeStruct((TP, W_AQ + W_B), BF),
        grid=(BATCH,),
        in_specs=[pl.BlockSpec(memory_space=pltpu.SMEM),
                  spec(W_AQ), spec(W_AKV), spec(W_AKV), spec(W_B), spec(W_B), spec(W_B)],
        out_specs=spec(W_AQ + W_B),
        compiler_params=_cparams(("parallel",)),
        name="attn_ctx",
    )(sink, qa, ka, va, qb, kb, vb)


QB = 128
SPAN = QB + 2 * WINDOW


def _attn_win_kernel(sink_ref, q_ref, k_ref, v_ref, ck_ref, cv_ref, o_ref):
    i = pl.program_id(1)
    start = i * QB
    s0 = pl.multiple_of(jnp.clip(start - WINDOW, 0, DEC_SEQ - SPAN), QB)
    qpos = start + lax.broadcasted_iota(jnp.int32, (GQA * QB, SPAN), 0) % QB
    kpos = s0 + lax.broadcasted_iota(jnp.int32, (GQA * QB, SPAN), 1)
    valid = jnp.abs(qpos - kpos) <= WINDOW
    q = q_ref[...]
    k = k_ref[pl.ds(s0, SPAN), :]
    v = v_ref[pl.ds(s0, SPAN), :]
    ck, cv = ck_ref[...], cv_ref[...]
    outs = []
    for kv in range(N_KV_A):
        sl = slice(kv * HD, (kv + 1) * HD)
        qg = _group_rows(q, kv)
        s_loc = jnp.where(valid, _dot_nt(qg, k[:, sl]) * SCALE, NEG_INF)
        s_ctx = _dot_nt(qg, ck[:, sl]) * SCALE
        p_loc, p_ctx = _softmax_parts([s_loc, s_ctx], _sink_column(sink_ref, kv, QB))
        outs += _ungroup_rows(_dot(p_loc, v[:, sl]) + _dot(p_ctx, cv[:, sl]), QB)
    o_ref[...] = jnp.concatenate(outs, axis=1).astype(BF)


def _attn_win(sink, qa, ka, va, ck, cv):
    nb = DEC_SEQ // QB
    return pl.pallas_call(
        _attn_win_kernel,
        out_shape=jax.ShapeDtypeStruct((TS, W_AQ), BF),
        grid=(DEC_BATCH, nb),
        in_specs=[
            pl.BlockSpec(memory_space=pltpu.SMEM),
            pl.BlockSpec((QB, W_AQ), lambda b, i: (b * nb + i, 0)),
            pl.BlockSpec((DEC_SEQ, W_AKV), lambda b, i: (b, 0)),
            pl.BlockSpec((DEC_SEQ, W_AKV), lambda b, i: (b, 0)),
            pl.BlockSpec((None, PAST, W_AKV), lambda b, i: (b, 0, 0)),
            pl.BlockSpec((None, PAST, W_AKV), lambda b, i: (b, 0, 0)),
        ],
        out_specs=pl.BlockSpec((QB, W_AQ), lambda b, i: (b * nb + i, 0)),
        compiler_params=_cparams(("parallel", "arbitrary")),
        name="attn_win",
    )(sink, qa, ka, va, ck, cv)


NB_HG = 4
NB_KEYS = NA_ROWS * GRID_W


def _row_start(r):
    return jnp.clip(r - NA_ROWS // 2, 0, GRID_H - NA_ROWS)


def _attn_nbr_kernel(q_ref, k_ref, v_ref, ck_ref, cv_ref, bias_ref, o_ref):
    r = pl.program_id(2)
    k0 = pl.multiple_of(_row_start(r) * GRID_W, GRID_W)
    q = q_ref[...]
    k = k_ref[pl.ds(k0, NB_KEYS), :]
    v = v_ref[pl.ds(k0, NB_KEYS), :]
    ck, cv = ck_ref[...], cv_ref[...]
    outs = []
    for h in range(NB_HG):
        sl = slice(h * HD, (h + 1) * HD)
        s_loc = _dot_nt(q[:, sl], k[:, sl]) * SCALE + bias_ref[h]
        s_ctx = _dot_nt(q[:, sl], ck[:, sl]) * SCALE
        p_loc, p_ctx = _softmax_parts([s_loc, s_ctx], None)
        outs.append(_dot(p_loc, v[:, sl]) + _dot(p_ctx, cv[:, sl]))
    o_ref[...] = jnp.concatenate(outs, axis=1).astype(BF)


def _attn_nbr(qb, kb, vb, ck, cv, bias):
    hw = NB_HG * HD
    n_hg = N_H_B // NB_HG
    return pl.pallas_call(
        _attn_nbr_kernel,
        out_shape=jax.ShapeDtypeStruct((TS, W_B), BF),
        grid=(DEC_BATCH, n_hg, GRID_H),
        in_specs=[
            pl.BlockSpec((GRID_W, hw), lambda b, g, r: (b * GRID_H + r, g)),
            pl.BlockSpec((DEC_SEQ, hw), lambda b, g, r: (b, g)),
            pl.BlockSpec((DEC_SEQ, hw), lambda b, g, r: (b, g)),
            pl.BlockSpec((None, PAST, hw), lambda b, g, r: (b, 0, g)),
            pl.BlockSpec((None, PAST, hw), lambda b, g, r: (b, 0, g)),
            pl.BlockSpec((None, NB_HG, GRID_W, NB_KEYS), lambda b, g, r: (r - _row_start(r), g, 0, 0)),
        ],
        out_specs=pl.BlockSpec((GRID_W, hw), lambda b, g, r: (b * GRID_H + r, g)),
        compiler_params=_cparams(("parallel", "parallel", "arbitrary")),
        name="attn_nbr",
    )(qb, kb, vb, ck, cv, bias)


def _rope_tables():
    pairs = HD // 4
    t = np.arange(DEC_SEQ)
    pos = np.stack([t // GRID_W, t % GRID_W], axis=-1).astype(np.float32)
    inv = jnp.asarray(ROPE_BASE, F32) ** (-jnp.arange(pairs, dtype=F32) * 2.0 / (HD // 2))
    ang = jnp.asarray(pos)[:, :, None] * inv
    cos = jnp.cos(ang)
    sin = jnp.sin(ang)
    cos_l = jnp.concatenate([cos, cos], axis=-1).reshape(DEC_SEQ, HD)
    sin_l = jnp.concatenate([-sin, sin], axis=-1).reshape(DEC_SEQ, HD)
    return cos_l, sin_l


def _nbr_bias_table(rpb):
    edge = GRID_W - NA_COLS
    ext = jnp.concatenate([jnp.repeat(rpb[..., :1], edge, axis=-1), rpb.astype(F32),
                           jnp.repeat(rpb[..., -1:], edge, axis=-1)], axis=-1)
    toep = jnp.stack([ext[..., GRID_W - 1 - q: 2 * GRID_W - 1 - q] for q in range(GRID_W)], axis=-2)
    qc = np.arange(GRID_W)
    cs = np.clip(qc - NA_COLS // 2, 0, GRID_W - NA_COLS)
    col_ok = (qc[None, :] >= cs[:, None]) & (qc[None, :] < cs[:, None] + NA_COLS)
    toep = jnp.where(jnp.asarray(col_ok), toep, NEG_INF)
    tabs = []
    for s in range(NA_ROWS):
        blk = toep[:, NA_ROWS - 1 - s: 2 * NA_ROWS - 1 - s]
        tabs.append(jnp.transpose(blk, (0, 2, 1, 3)).reshape(N_H_B, GRID_W, NB_KEYS))
    return jnp.stack(tabs, axis=0)


def kernel(x_prompt, x_sample, cache_a_k, cache_a_v, cache_b_k, cache_b_v, c, c_ctx,
           mod_w, mod_b, norm_ffn1, norm_mix, norm_ffn2, ffn1_w13, ffn1_w2, ffn2_w13, ffn2_w2,
           w_in, w_out, a_q_norm, a_k_norm, a_sink, b_q_norm, b_k_norm, b_rpb, conv_w, conv_b):
    hp = x_prompt.reshape(TP, D)
    hs = x_sample.reshape(TS, D)

    cvec = jnp.concatenate([c_ctx[None, :], c, jnp.zeros((MOD_ROWS - N_GROUPS, D), F32)], axis=0)
    mods = _modulation(cvec, mod_w, mod_b)
    mods = mods[:, :N_GROUPS].reshape(DEPTH * N_GROUPS * N_MOD, 1, D)

    cos_l, sin_l = _rope_tables()
    pad_rows = ((0, 0), (0, FF_PAD - D_FF), (0, 0))
    ffn1_w2p = jnp.pad(ffn1_w2.astype(BF), pad_rows)
    ffn2_w2p = jnp.pad(ffn2_w2.astype(BF), pad_rows)

    def ffn(h, latent, l, g, w13, w2p, k0):
        xn = _norm_mod(h, g[l][None, :], mods, l, k0, latent)
        return _ffn_down(_ffn_up(xn, w13, l), w2p, h, mods, l, k0 + 2, latent)

    new_ak, new_av, new_bk, new_bv = [], [], [], []
    for l in range(DEPTH):
        hp = ffn(hp, False, l, norm_ffn1, ffn1_w13, ffn1_w2p, 0)
        hs = ffn(hs, True, l, norm_ffn1, ffn1_w13, ffn1_w2p, 0)

        nrm = jnp.concatenate([a_q_norm[l][None], a_k_norm[l][None], b_q_norm[l][None], b_k_norm[l][None],
                               jnp.zeros((4, HD), F32)], axis=0)
        cw = jnp.concatenate([conv_w[l], jnp.zeros((5, CONV_CH), F32)], axis=0)
        cb = conv_b[l][None, :]
        sink = a_sink[l]
        g_mix = norm_mix[l][None, :]

        pp = _in_proj(_norm_mod(hp, g_mix, mods, l, 3, False), w_in, l)
        qa, ka, va, qb, kb, vb, cak, cav, cbk, cbv = _prep_ctx(pp, nrm)
        new_ak.append(cak.reshape(BATCH, SEQ, N_KV_A, HD))
        new_av.append(cav.reshape(BATCH, SEQ, N_KV_A, HD))
        new_bk.append(cbk.reshape(BATCH, SEQ, N_H_B, HD))
        new_bv.append(cbv.reshape(BATCH, SEQ, N_H_B, HD))
        o_ab = _attn_ctx(sink, qa, ka, va, qb, kb, vb)
        o_c = _short_conv(pp, cw, cb, SEQ, CONV_CH)
        hp = _out_proj([o_ab, o_c], w_out, hp, mods, l, False)

        ps = _in_proj(_norm_mod(hs, g_mix, mods, l, 3, True), w_in, l)
        qa, ka, va, qb, kb, vb = _prep_lat(ps, nrm, cos_l, sin_l)
        o_a = _attn_win(sink, qa, ka, va,
                        cache_a_k[:, l].reshape(DEC_BATCH, PAST, W_AKV).astype(BF),
                        cache_a_v[:, l].reshape(DEC_BATCH, PAST, W_AKV).astype(BF))
        o_b = _attn_nbr(qb, kb, vb,
                        cache_b_k[:, l].reshape(DEC_BATCH, PAST, W_B).astype(BF),
                        cache_b_v[:, l].reshape(DEC_BATCH, PAST, W_B).astype(BF),
                        _nbr_bias_table(b_rpb[l]))
        o_c = _short_conv(ps, cw, cb, DEC_SEQ, 128)
        hs = _out_proj([o_a, o_b, o_c], w_out, hs, mods, l, True)

        hp = ffn(hp, False, l, norm_ffn2, ffn2_w13, ffn2_w2p, 6)
        hs = ffn(hs, True, l, norm_ffn2, ffn2_w13, ffn2_w2p, 6)

    return (hp.reshape(BATCH, SEQ, D), hs.reshape(DEC_BATCH, DEC_SEQ, D),
            jnp.stack(new_ak, axis=1), jnp.stack(new_av, axis=1),
            jnp.stack(new_bk, axis=1), jnp.stack(new_bv, axis=1))
```

```python
import functools

import jax
import jax.numpy as jnp
import numpy as np
from jax import lax
from jax.experimental import pallas as pl
from jax.experimental.pallas import tpu as pltpu

D = 4096
BATCH = 16
SEQ = 256
DEPTH = 2
DEC_BATCH = 2
DEC_SEQ = 4096
PAST = 512
GRID_W = 64
GRID_H = DEC_SEQ // GRID_W
HD = 128
N_Q_A = 12
N_KV_A = 4
GQA = N_Q_A // N_KV_A
WINDOW = 128
N_H_B = 12
NA_ROWS = 8
NA_COLS = 16
CONV_CH = 1024
D_FF = 11008
N_MOD = 9
ROPE_BASE = 10000.0
EPS = 1e-6
NEG_INF = -1e30
SCALE = HD ** -0.5

TP = BATCH * SEQ
TS = DEC_BATCH * DEC_SEQ
N_GROUPS = 1 + DEC_BATCH
W_AQ = N_Q_A * HD
W_AKV = N_KV_A * HD
W_B = N_H_B * HD
QKV_W = W_AQ + 2 * W_AKV + 3 * W_B
IN_W = QKV_W + 3 * CONV_CH
MIX_W = W_AQ + W_B + CONV_CH

FF_BLK = 256
N_FF_BLK = D_FF // FF_BLK
FF_PAD = (N_FF_BLK + 1) * FF_BLK

VMEM_LIMIT = 56 * 1024 * 1024

BF = jnp.bfloat16
F32 = jnp.float32


def _cparams(sem, vmem=VMEM_LIMIT):
    return pltpu.CompilerParams(dimension_semantics=sem, vmem_limit_bytes=vmem)


def _dot(a, b):
    return jnp.dot(a, b, preferred_element_type=F32)


def _dot_nt(a, b):
    return lax.dot_general(a, b, (((1,), (1,)), ((), ())), preferred_element_type=F32)


def _mod_index(layer, k, rows_per_block, latent):
    base = layer * N_GROUPS * N_MOD + k
    if not latent:
        return lambda i: base
    blocks_per_group = DEC_SEQ // rows_per_block
    return lambda i: base + (1 + i // blocks_per_group) * N_MOD


MOD_TN = 512
MOD_ROWS = 16


def _mod_kernel(c_ref, w_ref, b_ref, o_ref):
    c = c_ref[...]
    x = (c * jax.nn.sigmoid(c)).astype(BF)
    o_ref[...] = _dot(x, w_ref[...].astype(BF)) + b_ref[...]


def _modulation(cvec, mod_w, mod_b):
    n = N_MOD * D
    return pl.pallas_call(
        _mod_kernel,
        out_shape=jax.ShapeDtypeStruct((DEPTH, MOD_ROWS, n), F32),
        grid=(DEPTH, n // MOD_TN),
        in_specs=[
            pl.BlockSpec((MOD_ROWS, D), lambda l, j: (0, 0)),
            pl.BlockSpec((None, D, MOD_TN), lambda l, j: (l, 0, j)),
            pl.BlockSpec((None, 1, MOD_TN), lambda l, j: (l, 0, j)),
        ],
        out_specs=pl.BlockSpec((None, MOD_ROWS, MOD_TN), lambda l, j: (l, 0, j)),
        compiler_params=_cparams(("parallel", "parallel")),
        name="modulation",
    )(cvec, mod_w, mod_b.reshape(DEPTH, 1, n))


NORM_TM = 512


def _norm_kernel(h_ref, g_ref, shift_ref, scale_ref, o_ref):
    x = h_ref[...]
    ms = jnp.mean(x * x, axis=-1, keepdims=True)
    y = x * lax.rsqrt(ms + EPS) * g_ref[...]
    o_ref[...] = (y * (1.0 + scale_ref[...]) + shift_ref[...]).astype(BF)


def _norm_mod(h, g, mods, layer, k_shift, latent):
    rows = h.shape[0]
    ish = _mod_index(layer, k_shift, NORM_TM, latent)
    isc = _mod_index(layer, k_shift + 1, NORM_TM, latent)
    return pl.pallas_call(
        _norm_kernel,
        out_shape=jax.ShapeDtypeStruct((rows, D), BF),
        grid=(rows // NORM_TM,),
        in_specs=[
            pl.BlockSpec((NORM_TM, D), lambda i: (i, 0)),
            pl.BlockSpec((1, D), lambda i: (0, 0)),
            pl.BlockSpec((None, 1, D), lambda i: (ish(i), 0, 0)),
            pl.BlockSpec((None, 1, D), lambda i: (isc(i), 0, 0)),
        ],
        out_specs=pl.BlockSpec((NORM_TM, D), lambda i: (i, 0)),
        compiler_params=_cparams(("parallel",)),
        name="norm_mod",
    )(h, g, mods, mods)


MM_TM = 2048


def _resident_lhs_spec(tm, width):
    return pl.BlockSpec((tm, width), lambda i, j: (i, 0), pipeline_mode=pl.Buffered(1))


def _ffn_up_kernel(x_ref, w1_ref, w3_ref, o_ref):
    j = pl.program_id(1)

    @pl.when(j < N_FF_BLK)
    def _():
        x = x_ref[...]
        a = _dot(x, w1_ref[...].astype(BF))
        b = _dot(x, w3_ref[...].astype(BF))
        o_ref[...] = (a * jax.nn.sigmoid(a) * b).astype(BF)

    @pl.when(j == N_FF_BLK)
    def _():
        o_ref[...] = jnp.zeros_like(o_ref)


def _ffn_up(xn, w13, layer):
    rows = xn.shape[0]
    last = N_FF_BLK - 1
    return pl.pallas_call(
        _ffn_up_kernel,
        out_shape=jax.ShapeDtypeStruct((rows, FF_PAD), BF),
        grid=(rows // MM_TM, N_FF_BLK + 1),
        in_specs=[
            _resident_lhs_spec(MM_TM, D),
            pl.BlockSpec((None, D, FF_BLK), lambda i, j: (layer, 0, jnp.minimum(j, last))),
            pl.BlockSpec((None, D, FF_BLK), lambda i, j: (layer, 0, N_FF_BLK + jnp.minimum(j, last))),
        ],
        out_specs=pl.BlockSpec((MM_TM, FF_BLK), lambda i, j: (i, j)),
        compiler_params=_cparams(("parallel", "arbitrary")),
        name="ffn_up",
    )(xn, w13, w13)


def _cast_pad_kernel(w_ref, o_ref):
    k = pl.program_id(1)

    @pl.when(k < N_FF_BLK)
    def _():
        o_ref[...] = w_ref[...].astype(BF)

    @pl.when(k == N_FF_BLK)
    def _():
        o_ref[...] = jnp.zeros_like(o_ref)


def _cast_pad_w2(w2):
    last = N_FF_BLK - 1
    return pl.pallas_call(
        _cast_pad_kernel,
        out_shape=jax.ShapeDtypeStruct((DEPTH, FF_PAD, D), BF),
        grid=(DEPTH, N_FF_BLK + 1),
        in_specs=[pl.BlockSpec((None, FF_BLK, D), lambda l, k: (l, jnp.minimum(k, last), 0))],
        out_specs=pl.BlockSpec((None, FF_BLK, D), lambda l, k: (l, k, 0)),
        compiler_params=_cparams(("parallel", "arbitrary")),
        name="cast_pad_w2",
    )(w2)


DOWN_TM = 1024
DOWN_TN = 2048
DOWN_TK = FF_PAD // 8


def _ffn_down_kernel(x_ref, w_ref, h_ref, gate_ref, o_ref):
    k = pl.program_id(2)
    last = pl.num_programs(2) - 1

    @pl.when(k == 0)
    def _():
        o_ref[...] = _dot(x_ref[...], w_ref[...])

    @pl.when(jnp.logical_and(k > 0, k < last))
    def _():
        o_ref[...] += _dot(x_ref[...], w_ref[...])

    @pl.when(k == last)
    def _():
        acc = o_ref[...] + _dot(x_ref[...], w_ref[...])
        o_ref[...] = h_ref[...] + (0.5 * gate_ref[...]) * acc


def _ffn_down(x, w2p, h, mods, layer, k_gate, latent):
    rows = x.shape[0]
    ig = _mod_index(layer, k_gate, DOWN_TM, latent)
    return pl.pallas_call(
        _ffn_down_kernel,
        out_shape=jax.ShapeDtypeStruct((rows, D), F32),
        grid=(rows // DOWN_TM, D // DOWN_TN, FF_PAD // DOWN_TK),
        in_specs=[
            pl.BlockSpec((DOWN_TM, DOWN_TK), lambda i, j, k: (i, k)),
            pl.BlockSpec((None, DOWN_TK, DOWN_TN), lambda i, j, k: (layer, k, j)),
            pl.BlockSpec((DOWN_TM, DOWN_TN), lambda i, j, k: (i, j)),
            pl.BlockSpec((None, 1, DOWN_TN), lambda i, j, k: (ig(i), 0, j)),
        ],
        out_specs=pl.BlockSpec((DOWN_TM, DOWN_TN), lambda i, j, k: (i, j)),
        compiler_params=_cparams(("parallel", "parallel", "arbitrary")),
        name="ffn_down",
    )(x, w2p, h, mods)


IN_TN = 512


def _in_proj_kernel(x_ref, w_ref, o_ref):
    o_ref[...] = _dot(x_ref[...], w_ref[...].astype(BF))


def _in_proj(u, w_in, layer):
    rows = u.shape[0]
    return pl.pallas_call(
        _in_proj_kernel,
        out_shape=jax.ShapeDtypeStruct((rows, IN_W), F32),
        grid=(rows // MM_TM, IN_W // IN_TN),
        in_specs=[
            _resident_lhs_spec(MM_TM, D),
            pl.BlockSpec((None, D, IN_TN), lambda i, j: (layer, 0, j)),
        ],
        out_specs=pl.BlockSpec((MM_TM, IN_TN), lambda i, j: (i, j)),
        compiler_params=_cparams(("parallel", "arbitrary")),
        name="in_proj",
    )(u, w_in)


OUT_TN = 256


def _out_proj_kernel(*refs, n_lhs):
    xs, ws = refs[:n_lhs], refs[n_lhs:2 * n_lhs]
    h_ref, gate_ref, o_ref = refs[2 * n_lhs:]
    acc = _dot(xs[0][...], ws[0][...].astype(BF))
    for x_ref, w_ref in zip(xs[1:], ws[1:]):
        acc += _dot(x_ref[...], w_ref[...].astype(BF))
    o_ref[...] = h_ref[...] + gate_ref[...] * acc


def _out_proj(xs, w_out, h, mods, layer, latent):
    rows = h.shape[0]
    ig = _mod_index(layer, 5, MM_TM, latent)
    x_specs, w_specs, c0 = [], [], 0
    for x in xs:
        w = x.shape[1]
        assert c0 % w == 0
        x_specs.append(_resident_lhs_spec(MM_TM, w))
        w_specs.append(pl.BlockSpec((None, w, OUT_TN), functools.partial(lambda i, j, r: (layer, r, j), r=c0 // w)))
        c0 += w
    assert c0 == MIX_W
    return pl.pallas_call(
        functools.partial(_out_proj_kernel, n_lhs=len(xs)),
        out_shape=jax.ShapeDtypeStruct((rows, D), F32),
        grid=(rows // MM_TM, D // OUT_TN),
        in_specs=x_specs + w_specs + [
            pl.BlockSpec((MM_TM, OUT_TN), lambda i, j: (i, j)),
            pl.BlockSpec((None, 1, OUT_TN), lambda i, j: (ig(i), 0, j)),
        ],
        out_specs=pl.BlockSpec((MM_TM, OUT_TN), lambda i, j: (i, j)),
        compiler_params=_cparams(("parallel", "arbitrary")),
        name="out_proj",
    )(*xs, *([w_out] * len(xs)), h, mods)


PREP_TM = 256
C_QA = 0
C_KA = C_QA + W_AQ
C_VA = C_KA + W_AKV
C_QB = C_VA + W_AKV
C_KB = C_QB + W_B
C_VB = C_KB + W_B


def _head_rms(x, g):
    ms = jnp.mean(x * x, axis=-1, keepdims=True)
    return x * lax.rsqrt(ms + EPS) * g


def _rope(x, cos, sin_signed, lo_lane):
    swapped = jnp.where(lo_lane, pltpu.roll(x, HD - 32, axis=1), pltpu.roll(x, 32, axis=1))
    return x * cos + swapped * sin_signed


def _prep_body(p_ref, nrm_ref, qa_ref, ka_ref, va_ref, qb_ref, kb_ref, vb_ref, rope, caches):
    g_aq = nrm_ref[0:1, :]
    g_ak = nrm_ref[1:2, :]
    g_bq = nrm_ref[2:3, :]
    g_bk = nrm_ref[3:4, :]
    if rope is not None:
        cos, sin_signed = rope
        lo_lane = (lax.broadcasted_iota(jnp.int32, (PREP_TM, HD), 1) % 64) < 32

    def col(c0, h):
        return p_ref[:, c0 + h * HD: c0 + (h + 1) * HD]

    for h in range(N_Q_A):
        y = _head_rms(col(C_QA, h), g_aq)
        if rope is not None:
            y = _rope(y, cos, sin_signed, lo_lane)
        qa_ref[:, h * HD:(h + 1) * HD] = y.astype(BF)
    for h in range(N_KV_A):
        y = _head_rms(col(C_KA, h), g_ak)
        if caches is not None:
            caches[0][:, h * HD:(h + 1) * HD] = y
        if rope is not None:
            y = _rope(y, cos, sin_signed, lo_lane)
        ka_ref[:, h * HD:(h + 1) * HD] = y.astype(BF)
    va = p_ref[:, C_VA:C_VA + W_AKV]
    va_ref[...] = va.astype(BF)
    if caches is not None:
        caches[1][...] = va
    for h in range(N_H_B):
        qb_ref[:, h * HD:(h + 1) * HD] = _head_rms(col(C_QB, h), g_bq).astype(BF)
        y = _head_rms(col(C_KB, h), g_bk)
        if caches is not None:
            caches[2][:, h * HD:(h + 1) * HD] = y
        kb_ref[:, h * HD:(h + 1) * HD] = y.astype(BF)
    vb = p_ref[:, C_VB:C_VB + W_B]
    vb_ref[...] = vb.astype(BF)
    if caches is not None:
        caches[3][...] = vb


def _prep_ctx_kernel(p_ref, nrm_ref, qa_ref, ka_ref, va_ref, qb_ref, kb_ref, vb_ref,
                     cak_ref, cav_ref, cbk_ref, cbv_ref):
    _prep_body(p_ref, nrm_ref, qa_ref, ka_ref, va_ref, qb_ref, kb_ref, vb_ref,
               None, (cak_ref, cav_ref, cbk_ref, cbv_ref))


def _prep_lat_kernel(p_ref, nrm_ref, cos_ref, sin_ref, qa_ref, ka_ref, va_ref, qb_ref, kb_ref, vb_ref):
    _prep_body(p_ref, nrm_ref, qa_ref, ka_ref, va_ref, qb_ref, kb_ref, vb_ref,
               (cos_ref[...], sin_ref[...]), None)


def _qkv_specs(rows):
    widths = (W_AQ, W_AKV, W_AKV, W_B, W_B, W_B)
    shapes = [jax.ShapeDtypeStruct((rows, w), BF) for w in widths]
    specs = [pl.BlockSpec((PREP_TM, w), lambda i: (i, 0)) for w in widths]
    return shapes, specs


def _prep_ctx(p, nrm):
    shapes, specs = _qkv_specs(TP)
    cache_w = (W_AKV, W_AKV, W_B, W_B)
    shapes += [jax.ShapeDtypeStruct((TP, w), F32) for w in cache_w]
    specs += [pl.BlockSpec((PREP_TM, w), lambda i: (i, 0)) for w in cache_w]
    return pl.pallas_call(
        _prep_ctx_kernel,
        out_shape=shapes,
        grid=(TP // PREP_TM,),
        in_specs=[
            pl.BlockSpec((PREP_TM, QKV_W), lambda i: (i, 0)),
            pl.BlockSpec((8, HD), lambda i: (0, 0)),
        ],
        out_specs=specs,
        compiler_params=_cparams(("parallel",)),
        name="prep_ctx",
    )(p, nrm)


def _prep_lat(p, nrm, cos, sin_signed):
    shapes, specs = _qkv_specs(TS)
    per_seq = DEC_SEQ // PREP_TM
    return pl.pallas_call(
        _prep_lat_kernel,
        out_shape=shapes,
        grid=(TS // PREP_TM,),
        in_specs=[
            pl.BlockSpec((PREP_TM, QKV_W), lambda i: (i, 0)),
            pl.BlockSpec((8, HD), lambda i: (0, 0)),
            pl.BlockSpec((PREP_TM, HD), lambda i: (i % per_seq, 0)),
            pl.BlockSpec((PREP_TM, HD), lambda i: (i % per_seq, 0)),
        ],
        out_specs=specs,
        compiler_params=_cparams(("parallel",)),
        name="prep_lat",
    )(p, nrm, cos, sin_signed)


def _conv_kernel(xc_ref, gb_ref, gc_ref, w_ref, b_ref, o_ref):
    u = gc_ref[...] * xc_ref[...]
    n = u.shape[0]
    row = lax.broadcasted_iota(jnp.int32, u.shape, 0)
    prev = jnp.where(row == 0, 0.0, pltpu.roll(u, 1, axis=0))
    nxt = jnp.where(row == n - 1, 0.0, pltpu.roll(u, n - 1, axis=0))
    y = w_ref[0:1, :] * prev + w_ref[1:2, :] * u + w_ref[2:3, :] * nxt + b_ref[...]
    o_ref[...] = (gb_ref[...] * y).astype(BF)


def _short_conv(p, conv_w, conv_b, rows, ch):
    n_seq = p.shape[0] // rows
    c0 = QKV_W // ch
    per = CONV_CH // ch
    return pl.pallas_call(
        _conv_kernel,
        out_shape=jax.ShapeDtypeStruct((n_seq * rows, CONV_CH), BF),
        grid=(n_seq, per),
        in_specs=[
            pl.BlockSpec((rows, ch), lambda s, c: (s, c0 + c)),
            pl.BlockSpec((rows, ch), lambda s, c: (s, c0 + per + c)),
            pl.BlockSpec((rows, ch), lambda s, c: (s, c0 + 2 * per + c)),
            pl.BlockSpec((8, ch), lambda s, c: (0, c)),
            pl.BlockSpec((1, ch), lambda s, c: (0, c)),
        ],
        out_specs=pl.BlockSpec((rows, ch), lambda s, c: (s, c)),
        compiler_params=_cparams(("parallel", "parallel")),
        name="short_conv",
    )(p, p, p, conv_w, conv_b)


def _softmax_parts(parts, sink):
    m = parts[0].max(axis=-1, keepdims=True)
    for s in parts[1:]:
        m = jnp.maximum(m, s.max(axis=-1, keepdims=True))
    if sink is not None:
        m = jnp.maximum(m, sink)
    es = [jnp.exp(s - m) for s in parts]
    den = es[0].sum(axis=-1, keepdims=True)
    for e in es[1:]:
        den = den + e.sum(axis=-1, keepdims=True)
    if sink is not None:
        den = den + jnp.exp(sink - m)
    inv = 1.0 / den
    return [(e * inv).astype(BF) for e in es]


def _sink_column(sink_ref, kv, rows):
    return jnp.concatenate(
        [jnp.full((rows, 1), sink_ref[kv * GQA + g], F32) for g in range(GQA)], axis=0)


def _group_rows(x, kv):
    return jnp.concatenate([x[:, (kv * GQA + g) * HD:(kv * GQA + g + 1) * HD] for g in range(GQA)], axis=0)


def _ungroup_rows(o, rows):
    return [o[g * rows:(g + 1) * rows] for g in range(GQA)]


def _attn_ctx_kernel(sink_ref, qa_ref, ka_ref, va_ref, qb_ref, kb_ref, vb_ref, o_ref):
    qa, ka, va = qa_ref[...], ka_ref[...], va_ref[...]
    qb, kb, vb = qb_ref[...], kb_ref[...], vb_ref[...]
    sls = [slice(h * HD, (h + 1) * HD) for h in range(N_H_B)]
    s_a = [_dot_nt(_group_rows(qa, kv), ka[:, sls[kv]]) * SCALE for kv in range(N_KV_A)]
    s_b = [_dot_nt(qb[:, sl], kb[:, sl]) * SCALE for sl in sls]
    p_a = [_softmax_parts([s], _sink_column(sink_ref, kv, SEQ))[0] for kv, s in enumerate(s_a)]
    p_b = [_softmax_parts([s], None)[0] for s in s_b]
    outs = []
    for kv, p in enumerate(p_a):
        outs += _ungroup_rows(_dot(p, va[:, sls[kv]]), SEQ)
    outs += [_dot(p, vb[:, sl]) for p, sl in zip(p_b, sls)]
    o_ref[...] = jnp.concatenate(outs, axis=1).astype(BF)


def _attn_ctx(sink, qa, ka, va, qb, kb, vb):
    def spec(w):
        return pl.BlockSpec((SEQ, w), lambda b: (b, 0))
    return pl.pallas_call(
        _attn_ctx_kernel,
        out_shape=jax.ShapeDtypeStruct((TP, W_AQ + W_B), BF),
        grid=(BATCH,),
        in_specs=[pl.BlockSpec(memory_space=pltpu.SMEM),
                  spec(W_AQ), spec(W_AKV), spec(W_AKV), spec(W_B), spec(W_B), spec(W_B)],
        out_specs=spec(W_AQ + W_B),
        compiler_params=_cparams(("parallel",)),
        name="attn_ctx",
    )(sink, qa, ka, va, qb, kb, vb)


QB = 128
SPAN = QB + 2 * WINDOW


def _attn_win_kernel(sink_ref, q_ref, k_ref, v_ref, ck_ref, cv_ref, o_ref):
    i = pl.program_id(1)
    start = i * QB
    s0 = pl.multiple_of(jnp.clip(start - WINDOW, 0, DEC_SEQ - SPAN), QB)
    qpos = start + lax.broadcasted_iota(jnp.int32, (GQA * QB, SPAN), 0) % QB
    kpos = s0 + lax.broadcasted_iota(jnp.int32, (GQA * QB, SPAN), 1)
    valid = jnp.abs(qpos - kpos) <= WINDOW
    q = q_ref[...]
    k = k_ref[pl.ds(s0, SPAN), :]
    v = v_ref[pl.ds(s0, SPAN), :]
    ck, cv = ck_ref[...], cv_ref[...]
    sls = [slice(kv * HD, (kv + 1) * HD) for kv in range(N_KV_A)]
    qgs = [_group_rows(q, kv) for kv in range(N_KV_A)]
    s_loc = [jnp.where(valid, _dot_nt(qg, k[:, sl]) * SCALE, NEG_INF) for qg, sl in zip(qgs, sls)]
    s_ctx = [_dot_nt(qg, ck[:, sl]) * SCALE for qg, sl in zip(qgs, sls)]
    ps = [_softmax_parts([s_loc[kv], s_ctx[kv]], _sink_column(sink_ref, kv, QB)) for kv in range(N_KV_A)]
    outs = []
    for (p_loc, p_ctx), sl in zip(ps, sls):
        outs += _ungroup_rows(_dot(p_loc, v[:, sl]) + _dot(p_ctx, cv[:, sl]), QB)
    o_ref[...] = jnp.concatenate(outs, axis=1).astype(BF)


def _attn_win(sink, qa, ka, va, ck, cv):
    nb = DEC_SEQ // QB
    return pl.pallas_call(
        _attn_win_kernel,
        out_shape=jax.ShapeDtypeStruct((TS, W_AQ), BF),
        grid=(DEC_BATCH, nb),
        in_specs=[
            pl.BlockSpec(memory_space=pltpu.SMEM),
            pl.BlockSpec((QB, W_AQ), lambda b, i: (b * nb + i, 0)),
            pl.BlockSpec((DEC_SEQ, W_AKV), lambda b, i: (b, 0)),
            pl.BlockSpec((DEC_SEQ, W_AKV), lambda b, i: (b, 0)),
            pl.BlockSpec((None, PAST, W_AKV), lambda b, i: (b, 0, 0)),
            pl.BlockSpec((None, PAST, W_AKV), lambda b, i: (b, 0, 0)),
        ],
        out_specs=pl.BlockSpec((QB, W_AQ), lambda b, i: (b * nb + i, 0)),
        compiler_params=_cparams(("parallel", "arbitrary")),
        name="attn_win",
    )(sink, qa, ka, va, ck, cv)


NB_HG = 6
NB_KEYS = NA_ROWS * GRID_W


def _row_start(r):
    return jnp.clip(r - NA_ROWS // 2, 0, GRID_H - NA_ROWS)


def _attn_nbr_kernel(q_ref, k_ref, v_ref, ck_ref, cv_ref, bias_ref, o_ref):
    r = pl.program_id(2)
    k0 = pl.multiple_of(_row_start(r) * GRID_W, GRID_W)
    q = q_ref[...]
    k = k_ref[pl.ds(k0, NB_KEYS), :]
    v = v_ref[pl.ds(k0, NB_KEYS), :]
    ck, cv = ck_ref[...], cv_ref[...]
    sls = [slice(h * HD, (h + 1) * HD) for h in range(NB_HG)]
    s_loc = [_dot_nt(q[:, sl], k[:, sl]) * SCALE + bias_ref[h] for h, sl in enumerate(sls)]
    s_ctx = [_dot_nt(q[:, sl], ck[:, sl]) * SCALE for sl in sls]
    ps = [_softmax_parts([a, b], None) for a, b in zip(s_loc, s_ctx)]
    outs = [_dot(p_loc, v[:, sl]) + _dot(p_ctx, cv[:, sl]) for (p_loc, p_ctx), sl in zip(ps, sls)]
    o_ref[...] = jnp.concatenate(outs, axis=1).astype(BF)


def _attn_nbr(qb, kb, vb, ck, cv, bias):
    hw = NB_HG * HD
    n_hg = N_H_B // NB_HG
    return pl.pallas_call(
        _attn_nbr_kernel,
        out_shape=jax.ShapeDtypeStruct((TS, W_B), BF),
        grid=(DEC_BATCH, n_hg, GRID_H),
        in_specs=[
            pl.BlockSpec((GRID_W, hw), lambda b, g, r: (b * GRID_H + r, g)),
            pl.BlockSpec((DEC_SEQ, hw), lambda b, g, r: (b, g)),
            pl.BlockSpec((DEC_SEQ, hw), lambda b, g, r: (b, g)),
            pl.BlockSpec((None, PAST, hw), lambda b, g, r: (b, 0, g)),
            pl.BlockSpec((None, PAST, hw), lambda b, g, r: (b, 0, g)),
            pl.BlockSpec((None, NB_HG, GRID_W, NB_KEYS), lambda b, g, r: (r - _row_start(r), g, 0, 0)),
        ],
        out_specs=pl.BlockSpec((GRID_W, hw), lambda b, g, r: (b * GRID_H + r, g)),
        compiler_params=_cparams(("parallel", "parallel", "arbitrary")),
        name="attn_nbr",
    )(qb, kb, vb, ck, cv, bias)


def _rope_tables():
    pairs = HD // 4
    t = np.arange(DEC_SEQ)
    pos = np.stack([t // GRID_W, t % GRID_W], axis=-1).astype(np.float32)
    inv = jnp.asarray(ROPE_BASE, F32) ** (-jnp.arange(pairs, dtype=F32) * 2.0 / (HD // 2))
    ang = jnp.asarray(pos)[:, :, None] * inv
    cos = jnp.cos(ang)
    sin = jnp.sin(ang)
    cos_l = jnp.concatenate([cos, cos], axis=-1).reshape(DEC_SEQ, HD)
    sin_l = jnp.concatenate([-sin, sin], axis=-1).reshape(DEC_SEQ, HD)
    return cos_l, sin_l


def _nbr_bias_table(rpb):
    edge = GRID_W - NA_COLS
    ext = jnp.concatenate([jnp.repeat(rpb[..., :1], edge, axis=-1), rpb.astype(F32),
                           jnp.repeat(rpb[..., -1:], edge, axis=-1)], axis=-1)
    toep = jnp.stack([ext[..., GRID_W - 1 - q: 2 * GRID_W - 1 - q] for q in range(GRID_W)], axis=-2)
    qc = np.arange(GRID_W)
    cs = np.clip(qc - NA_COLS // 2, 0, GRID_W - NA_COLS)
    col_ok = (qc[None, :] >= cs[:, None]) & (qc[None, :] < cs[:, None] + NA_COLS)
    toep = jnp.where(jnp.asarray(col_ok), toep, NEG_INF)
    tabs = []
    for s in range(NA_ROWS):
        blk = toep[:, NA_ROWS - 1 - s: 2 * NA_ROWS - 1 - s]
        tabs.append(jnp.transpose(blk, (0, 2, 1, 3)).reshape(N_H_B, GRID_W, NB_KEYS))
    return jnp.stack(tabs, axis=0)


def kernel(x_prompt, x_sample, cache_a_k, cache_a_v, cache_b_k, cache_b_v, c, c_ctx,
           mod_w, mod_b, norm_ffn1, norm_mix, norm_ffn2, ffn1_w13, ffn1_w2, ffn2_w13, ffn2_w2,
           w_in, w_out, a_q_norm, a_k_norm, a_sink, b_q_norm, b_k_norm, b_rpb, conv_w, conv_b):
    hp = x_prompt.reshape(TP, D)
    hs = x_sample.reshape(TS, D)

    cvec = jnp.concatenate([c_ctx[None, :], c, jnp.zeros((MOD_ROWS - N_GROUPS, D), F32)], axis=0)
    mods = _modulation(cvec, mod_w, mod_b)
    mods = mods[:, :N_GROUPS].reshape(DEPTH * N_GROUPS * N_MOD, 1, D)

    cos_l, sin_l = _rope_tables()
    ffn1_w2p = _cast_pad_w2(ffn1_w2)
    ffn2_w2p = _cast_pad_w2(ffn2_w2)

    def ffn(h, latent, l, g, w13, w2p, k0):
        xn = _norm_mod(h, g[l][None, :], mods, l, k0, latent)
        return _ffn_down(_ffn_up(xn, w13, l), w2p, h, mods, l, k0 + 2, latent)

    new_ak, new_av, new_bk, new_bv = [], [], [], []
    for l in range(DEPTH):
        hp = ffn(hp, False, l, norm_ffn1, ffn1_w13, ffn1_w2p, 0)
        hs = ffn(hs, True, l, norm_ffn1, ffn1_w13, ffn1_w2p, 0)

        nrm = jnp.concatenate([a_q_norm[l][None], a_k_norm[l][None], b_q_norm[l][None], b_k_norm[l][None],
                               jnp.zeros((4, HD), F32)], axis=0)
        cw = jnp.concatenate([conv_w[l], jnp.zeros((5, CONV_CH), F32)], axis=0)
        cb = conv_b[l][None, :]
        sink = a_sink[l]
        g_mix = norm_mix[l][None, :]

        pp = _in_proj(_norm_mod(hp, g_mix, mods, l, 3, False), w_in, l)
        qa, ka, va, qb, kb, vb, cak, cav, cbk, cbv = _prep_ctx(pp, nrm)
        new_ak.append(cak.reshape(BATCH, SEQ, N_KV_A, HD))
        new_av.append(cav.reshape(BATCH, SEQ, N_KV_A, HD))
        new_bk.append(cbk.reshape(BATCH, SEQ, N_H_B, HD))
        new_bv.append(cbv.reshape(BATCH, SEQ, N_H_B, HD))
        o_ab = _attn_ctx(sink, qa, ka, va, qb, kb, vb)
        o_c = _short_conv(pp, cw, cb, SEQ, CONV_CH)
        hp = _out_proj([o_ab, o_c], w_out, hp, mods, l, False)

        ps = _in_proj(_norm_mod(hs, g_mix, mods, l, 3, True), w_in, l)
        qa, ka, va, qb, kb, vb = _prep_lat(ps, nrm, cos_l, sin_l)
        o_a = _attn_win(sink, qa, ka, va,
                        cache_a_k[:, l].reshape(DEC_BATCH, PAST, W_AKV).astype(BF),
                        cache_a_v[:, l].reshape(DEC_BATCH, PAST, W_AKV).astype(BF))
        o_b = _attn_nbr(qb, kb, vb,
                        cache_b_k[:, l].reshape(DEC_BATCH, PAST, W_B).astype(BF),
                        cache_b_v[:, l].reshape(DEC_BATCH, PAST, W_B).astype(BF),
                        _nbr_bias_table(b_rpb[l]))
        o_c = _short_conv(ps, cw, cb, DEC_SEQ, 128)
        hs = _out_proj([o_a, o_b, o_c], w_out, hs, mods, l, True)

        hp = ffn(hp, False, l, norm_ffn2, ffn2_w13, ffn2_w2p, 6)
        hs = ffn(hs, True, l, norm_ffn2, ffn2_w13, ffn2_w2p, 6)

    return (hp.reshape(BATCH, SEQ, D), hs.reshape(DEC_BATCH, DEC_SEQ, D),
            jnp.stack(new_ak, axis=1), jnp.stack(new_av, axis=1),
            jnp.stack(new_bk, axis=1), jnp.stack(new_bv, axis=1))
```

```python
import functools

import jax
import jax.numpy as jnp
import numpy as np
from jax import lax
from jax.experimental import pallas as pl
from jax.experimental.pallas import tpu as pltpu

D = 4096
BATCH = 16
SEQ = 256
DEPTH = 2
DEC_BATCH = 2
DEC_SEQ = 4096
PAST = 512
GRID_W = 64
GRID_H = DEC_SEQ // GRID_W
HD = 128
N_Q_A = 12
N_KV_A = 4
GQA = N_Q_A // N_KV_A
WINDOW = 128
N_H_B = 12
NA_ROWS = 8
NA_COLS = 16
CONV_CH = 1024
D_FF = 11008
N_MOD = 9
ROPE_BASE = 10000.0
EPS = 1e-6
NEG_INF = -1e30
SCALE = HD ** -0.5

TP = BATCH * SEQ
TS = DEC_BATCH * DEC_SEQ
N_GROUPS = 1 + DEC_BATCH
W_AQ = N_Q_A * HD
W_AKV = N_KV_A * HD
W_B = N_H_B * HD
QKV_W = W_AQ + 2 * W_AKV + 3 * W_B
IN_W = QKV_W + 3 * CONV_CH
MIX_W = W_AQ + W_B + CONV_CH

FF_BLK = 256
N_FF_BLK = D_FF // FF_BLK
FF_PAD = (N_FF_BLK + 1) * FF_BLK

VMEM_LIMIT = 56 * 1024 * 1024

BF = jnp.bfloat16
F32 = jnp.float32


def _cparams(sem, vmem=VMEM_LIMIT):
    return pltpu.CompilerParams(dimension_semantics=sem, vmem_limit_bytes=vmem)


def _dot(a, b):
    return jnp.dot(a, b, preferred_element_type=F32)


def _dot_nt(a, b):
    return lax.dot_general(a, b, (((1,), (1,)), ((), ())), preferred_element_type=F32)


def _mod_index(layer, k, rows_per_block, latent):
    base = layer * N_GROUPS * N_MOD + k
    if not latent:
        return lambda i: base
    blocks_per_group = DEC_SEQ // rows_per_block
    return lambda i: base + (1 + i // blocks_per_group) * N_MOD


MOD_TN = 512
MOD_ROWS = 16


def _mod_kernel(c_ref, w_ref, b_ref, o_ref):
    c = c_ref[...]
    x = (c * jax.nn.sigmoid(c)).astype(BF)
    o_ref[...] = _dot(x, w_ref[...].astype(BF)) + b_ref[...]


def _modulation(cvec, mod_w, mod_b):
    n = N_MOD * D
    return pl.pallas_call(
        _mod_kernel,
        out_shape=jax.ShapeDtypeStruct((DEPTH, MOD_ROWS, n), F32),
        grid=(DEPTH, n // MOD_TN),
        in_specs=[
            pl.BlockSpec((MOD_ROWS, D), lambda l, j: (0, 0)),
            pl.BlockSpec((None, D, MOD_TN), lambda l, j: (l, 0, j)),
            pl.BlockSpec((None, 1, MOD_TN), lambda l, j: (l, 0, j)),
        ],
        out_specs=pl.BlockSpec((None, MOD_ROWS, MOD_TN), lambda l, j: (l, 0, j)),
        compiler_params=_cparams(("parallel", "parallel")),
        name="modulation",
    )(cvec, mod_w, mod_b.reshape(DEPTH, 1, n))


NORM_TM = 512


def _norm_kernel(h_ref, g_ref, shift_ref, scale_ref, o_ref):
    x = h_ref[...]
    ms = jnp.mean(x * x, axis=-1, keepdims=True)
    y = x * lax.rsqrt(ms + EPS) * g_ref[...]
    o_ref[...] = (y * (1.0 + scale_ref[...]) + shift_ref[...]).astype(BF)


def _norm_mod(h, g, mods, layer, k_shift, latent):
    rows = h.shape[0]
    ish = _mod_index(layer, k_shift, NORM_TM, latent)
    isc = _mod_index(layer, k_shift + 1, NORM_TM, latent)
    return pl.pallas_call(
        _norm_kernel,
        out_shape=jax.ShapeDtypeStruct((rows, D), BF),
        grid=(rows // NORM_TM,),
        in_specs=[
            pl.BlockSpec((NORM_TM, D), lambda i: (i, 0)),
            pl.BlockSpec((1, D), lambda i: (0, 0)),
            pl.BlockSpec((None, 1, D), lambda i: (ish(i), 0, 0)),
            pl.BlockSpec((None, 1, D), lambda i: (isc(i), 0, 0)),
        ],
        out_specs=pl.BlockSpec((NORM_TM, D), lambda i: (i, 0)),
        compiler_params=_cparams(("parallel",)),
        name="norm_mod",
    )(h, g, mods, mods)


MM_TM = 2048


def _resident_lhs_spec(tm, width):
    return pl.BlockSpec((tm, width), lambda i, j: (i, 0), pipeline_mode=pl.Buffered(1))


def _ffn_up_kernel(x_ref, w1_ref, w3_ref, o_ref):
    j = pl.program_id(1)

    @pl.when(j < N_FF_BLK)
    def _():
        x = x_ref[...]
        a = _dot(x, w1_ref[...].astype(BF))
        b = _dot(x, w3_ref[...].astype(BF))
        o_ref[...] = (a * jax.nn.sigmoid(a) * b).astype(BF)

    @pl.when(j == N_FF_BLK)
    def _():
        o_ref[...] = jnp.zeros_like(o_ref)


def _ffn_up(xn, w13, layer):
    rows = xn.shape[0]
    last = N_FF_BLK - 1
    return pl.pallas_call(
        _ffn_up_kernel,
        out_shape=jax.ShapeDtypeStruct((rows, FF_PAD), BF),
        grid=(rows // MM_TM, N_FF_BLK + 1),
        in_specs=[
            _resident_lhs_spec(MM_TM, D),
            pl.BlockSpec((None, D, FF_BLK), lambda i, j: (layer, 0, jnp.minimum(j, last))),
            pl.BlockSpec((None, D, FF_BLK), lambda i, j: (layer, 0, N_FF_BLK + jnp.minimum(j, last))),
        ],
        out_specs=pl.BlockSpec((MM_TM, FF_BLK), lambda i, j: (i, j)),
        compiler_params=_cparams(("parallel", "arbitrary")),
        name="ffn_up",
    )(xn, w13, w13)


def _cast_pad_kernel(w_ref, o_ref):
    k = pl.program_id(1)

    @pl.when(k < N_FF_BLK)
    def _():
        o_ref[...] = w_ref[...].astype(BF)

    @pl.when(k == N_FF_BLK)
    def _():
        o_ref[...] = jnp.zeros_like(o_ref)


def _cast_pad_w2(w2):
    last = N_FF_BLK - 1
    return pl.pallas_call(
        _cast_pad_kernel,
        out_shape=jax.ShapeDtypeStruct((DEPTH, FF_PAD, D), BF),
        grid=(DEPTH, N_FF_BLK + 1),
        in_specs=[pl.BlockSpec((None, FF_BLK, D), lambda l, k: (l, jnp.minimum(k, last), 0))],
        out_specs=pl.BlockSpec((None, FF_BLK, D), lambda l, k: (l, k, 0)),
        compiler_params=_cparams(("parallel", "arbitrary")),
        name="cast_pad_w2",
    )(w2)


DOWN_TM = 1024
DOWN_TN = 1024
DOWN_TK = FF_PAD // 4


def _ffn_down_kernel(x_ref, w_ref, h_ref, gate_ref, o_ref):
    k = pl.program_id(2)
    last = pl.num_programs(2) - 1

    @pl.when(k == 0)
    def _():
        o_ref[...] = _dot(x_ref[...], w_ref[...])

    @pl.when(jnp.logical_and(k > 0, k < last))
    def _():
        o_ref[...] += _dot(x_ref[...], w_ref[...])

    @pl.when(k == last)
    def _():
        acc = o_ref[...] + _dot(x_ref[...], w_ref[...])
        o_ref[...] = h_ref[...] + (0.5 * gate_ref[...]) * acc


def _ffn_down(x, w2p, h, mods, layer, k_gate, latent):
    rows = x.shape[0]
    ig = _mod_index(layer, k_gate, DOWN_TM, latent)
    return pl.pallas_call(
        _ffn_down_kernel,
        out_shape=jax.ShapeDtypeStruct((rows, D), F32),
        grid=(rows // DOWN_TM, D // DOWN_TN, FF_PAD // DOWN_TK),
        in_specs=[
            pl.BlockSpec((DOWN_TM, DOWN_TK), lambda i, j, k: (i, k)),
            pl.BlockSpec((None, DOWN_TK, DOWN_TN), lambda i, j, k: (layer, k, j)),
            pl.BlockSpec((DOWN_TM, DOWN_TN), lambda i, j, k: (i, j)),
            pl.BlockSpec((None, 1, DOWN_TN), lambda i, j, k: (ig(i), 0, j)),
        ],
        out_specs=pl.BlockSpec((DOWN_TM, DOWN_TN), lambda i, j, k: (i, j)),
        compiler_params=_cparams(("parallel", "parallel", "arbitrary")),
        name="ffn_down",
    )(x, w2p, h, mods)


IN_TN = 512


def _in_proj_kernel(x_ref, w_ref, o_ref):
    o_ref[...] = _dot(x_ref[...], w_ref[...].astype(BF))


def _in_proj(u, w_in, layer):
    rows = u.shape[0]
    return pl.pallas_call(
        _in_proj_kernel,
        out_shape=jax.ShapeDtypeStruct((rows, IN_W), F32),
        grid=(rows // MM_TM, IN_W // IN_TN),
        in_specs=[
            _resident_lhs_spec(MM_TM, D),
            pl.BlockSpec((None, D, IN_TN), lambda i, j: (layer, 0, j)),
        ],
        out_specs=pl.BlockSpec((MM_TM, IN_TN), lambda i, j: (i, j)),
        compiler_params=_cparams(("parallel", "arbitrary")),
        name="in_proj",
    )(u, w_in)


OUT_TN = 512


def _out_proj_kernel(*refs, n_lhs):
    xs, ws = refs[:n_lhs], refs[n_lhs:2 * n_lhs]
    h_ref, gate_ref, o_ref = refs[2 * n_lhs:]
    acc = _dot(xs[0][...], ws[0][...].astype(BF))
    for x_ref, w_ref in zip(xs[1:], ws[1:]):
        acc += _dot(x_ref[...], w_ref[...].astype(BF))
    o_ref[...] = h_ref[...] + gate_ref[...] * acc


def _out_proj(xs, w_out, h, mods, layer, latent):
    rows = h.shape[0]
    ig = _mod_index(layer, 5, MM_TM, latent)
    x_specs, w_specs, c0 = [], [], 0
    for x in xs:
        w = x.shape[1]
        assert c0 % w == 0
        x_specs.append(_resident_lhs_spec(MM_TM, w))
        w_specs.append(pl.BlockSpec((None, w, OUT_TN), functools.partial(lambda i, j, r: (layer, r, j), r=c0 // w)))
        c0 += w
    assert c0 == MIX_W
    return pl.pallas_call(
        functools.partial(_out_proj_kernel, n_lhs=len(xs)),
        out_shape=jax.ShapeDtypeStruct((rows, D), F32),
        grid=(rows // MM_TM, D // OUT_TN),
        in_specs=x_specs + w_specs + [
            pl.BlockSpec((MM_TM, OUT_TN), lambda i, j: (i, j)),
            pl.BlockSpec((None, 1, OUT_TN), lambda i, j: (ig(i), 0, j)),
        ],
        out_specs=pl.BlockSpec((MM_TM, OUT_TN), lambda i, j: (i, j)),
        compiler_params=_cparams(("parallel", "arbitrary")),
        name="out_proj",
    )(*xs, *([w_out] * len(xs)), h, mods)


PREP_TM = 256
C_QA = 0
C_KA = C_QA + W_AQ
C_VA = C_KA + W_AKV
C_QB = C_VA + W_AKV
C_KB = C_QB + W_B
C_VB = C_KB + W_B


def _head_rms(x, g):
    ms = jnp.mean(x * x, axis=-1, keepdims=True)
    return x * lax.rsqrt(ms + EPS) * g


def _rope(x, cos, sin_signed, lo_lane):
    swapped = jnp.where(lo_lane, pltpu.roll(x, HD - 32, axis=1), pltpu.roll(x, 32, axis=1))
    return x * cos + swapped * sin_signed


def _prep_body(p_ref, nrm_ref, qa_ref, ka_ref, va_ref, qb_ref, kb_ref, vb_ref, rope, caches):
    g_aq = nrm_ref[0:1, :]
    g_ak = nrm_ref[1:2, :]
    g_bq = nrm_ref[2:3, :]
    g_bk = nrm_ref[3:4, :]
    if rope is not None:
        cos, sin_signed = rope
        lo_lane = (lax.broadcasted_iota(jnp.int32, (PREP_TM, HD), 1) % 64) < 32

    def col(c0, h):
        return p_ref[:, c0 + h * HD: c0 + (h + 1) * HD]

    for h in range(N_Q_A):
        y = _head_rms(col(C_QA, h), g_aq)
        if rope is not None:
            y = _rope(y, cos, sin_signed, lo_lane)
        qa_ref[:, h * HD:(h + 1) * HD] = y.astype(BF)
    for h in range(N_KV_A):
        y = _head_rms(col(C_KA, h), g_ak)
        if caches is not None:
            caches[0][:, h * HD:(h + 1) * HD] = y
        if rope is not None:
            y = _rope(y, cos, sin_signed, lo_lane)
        ka_ref[:, h * HD:(h + 1) * HD] = y.astype(BF)
    va = p_ref[:, C_VA:C_VA + W_AKV]
    va_ref[...] = va.astype(BF)
    if caches is not None:
        caches[1][...] = va
    for h in range(N_H_B):
        qb_ref[:, h * HD:(h + 1) * HD] = _head_rms(col(C_QB, h), g_bq).astype(BF)
        y = _head_rms(col(C_KB, h), g_bk)
        if caches is not None:
            caches[2][:, h * HD:(h + 1) * HD] = y
        kb_ref[:, h * HD:(h + 1) * HD] = y.astype(BF)
    vb = p_ref[:, C_VB:C_VB + W_B]
    vb_ref[...] = vb.astype(BF)
    if caches is not None:
        caches[3][...] = vb


def _prep_ctx_kernel(p_ref, nrm_ref, qa_ref, ka_ref, va_ref, qb_ref, kb_ref, vb_ref,
                     cak_ref, cav_ref, cbk_ref, cbv_ref):
    _prep_body(p_ref, nrm_ref, qa_ref, ka_ref, va_ref, qb_ref, kb_ref, vb_ref,
               None, (cak_ref, cav_ref, cbk_ref, cbv_ref))


def _prep_lat_kernel(p_ref, nrm_ref, cos_ref, sin_ref, qa_ref, ka_ref, va_ref, qb_ref, kb_ref, vb_ref):
    _prep_body(p_ref, nrm_ref, qa_ref, ka_ref, va_ref, qb_ref, kb_ref, vb_ref,
               (cos_ref[...], sin_ref[...]), None)


def _qkv_specs(rows):
    widths = (W_AQ, W_AKV, W_AKV, W_B, W_B, W_B)
    shapes = [jax.ShapeDtypeStruct((rows, w), BF) for w in widths]
    specs = [pl.BlockSpec((PREP_TM, w), lambda i: (i, 0)) for w in widths]
    return shapes, specs


def _prep_ctx(p, nrm):
    shapes, specs = _qkv_specs(TP)
    cache_w = (W_AKV, W_AKV, W_B, W_B)
    shapes += [jax.ShapeDtypeStruct((TP, w), F32) for w in cache_w]
    specs += [pl.BlockSpec((PREP_TM, w), lambda i: (i, 0)) for w in cache_w]
    return pl.pallas_call(
        _prep_ctx_kernel,
        out_shape=shapes,
        grid=(TP // PREP_TM,),
        in_specs=[
            pl.BlockSpec((PREP_TM, QKV_W), lambda i: (i, 0)),
            pl.BlockSpec((8, HD), lambda i: (0, 0)),
        ],
        out_specs=specs,
        compiler_params=_cparams(("parallel",)),
        name="prep_ctx",
    )(p, nrm)


def _prep_lat(p, nrm, cos, sin_signed):
    shapes, specs = _qkv_specs(TS)
    per_seq = DEC_SEQ // PREP_TM
    return pl.pallas_call(
        _prep_lat_kernel,
        out_shape=shapes,
        grid=(TS // PREP_TM,),
        in_specs=[
            pl.BlockSpec((PREP_TM, QKV_W), lambda i: (i, 0)),
            pl.BlockSpec((8, HD), lambda i: (0, 0)),
            pl.BlockSpec((PREP_TM, HD), lambda i: (i % per_seq, 0)),
            pl.BlockSpec((PREP_TM, HD), lambda i: (i % per_seq, 0)),
        ],
        out_specs=specs,
        compiler_params=_cparams(("parallel",)),
        name="prep_lat",
    )(p, nrm, cos, sin_signed)


def _conv_kernel(xc_ref, gb_ref, gc_ref, w_ref, b_ref, o_ref):
    u = gc_ref[...] * xc_ref[...]
    n = u.shape[0]
    row = lax.broadcasted_iota(jnp.int32, u.shape, 0)
    prev = jnp.where(row == 0, 0.0, pltpu.roll(u, 1, axis=0))
    nxt = jnp.where(row == n - 1, 0.0, pltpu.roll(u, n - 1, axis=0))
    y = w_ref[0:1, :] * prev + w_ref[1:2, :] * u + w_ref[2:3, :] * nxt + b_ref[...]
    o_ref[...] = (gb_ref[...] * y).astype(BF)


def _short_conv(p, conv_w, conv_b, rows, ch):
    n_seq = p.shape[0] // rows
    c0 = QKV_W // ch
    per = CONV_CH // ch
    return pl.pallas_call(
        _conv_kernel,
        out_shape=jax.ShapeDtypeStruct((n_seq * rows, CONV_CH), BF),
        grid=(n_seq, per),
        in_specs=[
            pl.BlockSpec((rows, ch), lambda s, c: (s, c0 + c)),
            pl.BlockSpec((rows, ch), lambda s, c: (s, c0 + per + c)),
            pl.BlockSpec((rows, ch), lambda s, c: (s, c0 + 2 * per + c)),
            pl.BlockSpec((8, ch), lambda s, c: (0, c)),
            pl.BlockSpec((1, ch), lambda s, c: (0, c)),
        ],
        out_specs=pl.BlockSpec((rows, ch), lambda s, c: (s, c)),
        compiler_params=_cparams(("parallel", "parallel")),
        name="short_conv",
    )(p, p, p, conv_w, conv_b)


def _softmax_parts(parts, sink):
    m = parts[0].max(axis=-1, keepdims=True)
    for s in parts[1:]:
        m = jnp.maximum(m, s.max(axis=-1, keepdims=True))
    if sink is not None:
        m = jnp.maximum(m, sink)
    es = [jnp.exp(s - m) for s in parts]
    den = es[0].sum(axis=-1, keepdims=True)
    for e in es[1:]:
        den = den + e.sum(axis=-1, keepdims=True)
    if sink is not None:
        den = den + jnp.exp(sink - m)
    inv = 1.0 / den
    return [(e * inv).astype(BF) for e in es]


def _sink_column(sink_ref, kv, rows):
    return jnp.concatenate(
        [jnp.full((rows, 1), sink_ref[kv * GQA + g], F32) for g in range(GQA)], axis=0)


def _group_rows(x, kv):
    return jnp.concatenate([x[:, (kv * GQA + g) * HD:(kv * GQA + g + 1) * HD] for g in range(GQA)], axis=0)


def _ungroup_rows(o, rows):
    return [o[g * rows:(g + 1) * rows] for g in range(GQA)]


def _attn_ctx_kernel(sink_ref, qa_ref, ka_ref, va_ref, qb_ref, kb_ref, vb_ref, o_ref):
    qa, ka, va = qa_ref[...], ka_ref[...], va_ref[...]
    qb, kb, vb = qb_ref[...], kb_ref[...], vb_ref[...]
    sls = [slice(h * HD, (h + 1) * HD) for h in range(N_H_B)]
    s_a = [_dot_nt(_group_rows(qa, kv), ka[:, sls[kv]]) * SCALE for kv in range(N_KV_A)]
    s_b = [_dot_nt(qb[:, sl], kb[:, sl]) * SCALE for sl in sls]
    p_a = [_softmax_parts([s], _sink_column(sink_ref, kv, SEQ))[0] for kv, s in enumerate(s_a)]
    p_b = [_softmax_parts([s], None)[0] for s in s_b]
    outs = []
    for kv, p in enumerate(p_a):
        outs += _ungroup_rows(_dot(p, va[:, sls[kv]]), SEQ)
    outs += [_dot(p, vb[:, sl]) for p, sl in zip(p_b, sls)]
    o_ref[...] = jnp.concatenate(outs, axis=1).astype(BF)


def _attn_ctx(sink, qa, ka, va, qb, kb, vb):
    def spec(w):
        return pl.BlockSpec((SEQ, w), lambda b: (b, 0))
    return pl.pallas_call(
        _attn_ctx_kernel,
        out_shape=jax.ShapeDtypeStruct((TP, W_AQ + W_B), BF),
        grid=(BATCH,),
        in_specs=[pl.BlockSpec(memory_space=pltpu.SMEM),
                  spec(W_AQ), spec(W_AKV), spec(W_AKV), spec(W_B), spec(W_B), spec(W_B)],
        out_specs=spec(W_AQ + W_B),
        compiler_params=_cparams(("parallel",)),
        name="attn_ctx",
    )(sink, qa, ka, va, qb, kb, vb)


QB = 128
SPAN = QB + 2 * WINDOW


def _attn_win_kernel(sink_ref, q_ref, k_ref, v_ref, ck_ref, cv_ref, o_ref):
    i = pl.program_id(1)
    start = i * QB
    s0 = pl.multiple_of(jnp.clip(start - WINDOW, 0, DEC_SEQ - SPAN), QB)
    qpos = start + lax.broadcasted_iota(jnp.int32, (GQA * QB, SPAN), 0) % QB
    kpos = s0 + lax.broadcasted_iota(jnp.int32, (GQA * QB, SPAN), 1)
    valid = jnp.abs(qpos - kpos) <= WINDOW
    q = q_ref[...]
    k = k_ref[pl.ds(s0, SPAN), :]
    v = v_ref[pl.ds(s0, SPAN), :]
    kk = jnp.concatenate([k, ck_ref[...]], axis=0)
    vv = jnp.concatenate([v, cv_ref[...]], axis=0)
    sls = [slice(kv * HD, (kv + 1) * HD) for kv in range(N_KV_A)]
    ss = [_dot_nt(_group_rows(q, kv), kk[:, sl]) * SCALE for kv, sl in enumerate(sls)]
    ss = [jnp.concatenate([jnp.where(valid, s[:, :SPAN], NEG_INF), s[:, SPAN:]], axis=1) for s in ss]
    ps = [_softmax_parts([s], _sink_column(sink_ref, kv, QB))[0] for kv, s in enumerate(ss)]
    outs = []
    for p, sl in zip(ps, sls):
        outs += _ungroup_rows(_dot(p, vv[:, sl]), QB)
    o_ref[...] = jnp.concatenate(outs, axis=1).astype(BF)


def _attn_win(sink, qa, ka, va, ck, cv):
    nb = DEC_SEQ // QB
    return pl.pallas_call(
        _attn_win_kernel,
        out_shape=jax.ShapeDtypeStruct((TS, W_AQ), BF),
        grid=(DEC_BATCH, nb),
        in_specs=[
            pl.BlockSpec(memory_space=pltpu.SMEM),
            pl.BlockSpec((QB, W_AQ), lambda b, i: (b * nb + i, 0)),
            pl.BlockSpec((DEC_SEQ, W_AKV), lambda b, i: (b, 0)),
            pl.BlockSpec((DEC_SEQ, W_AKV), lambda b, i: (b, 0)),
            pl.BlockSpec((None, PAST, W_AKV), lambda b, i: (b, 0, 0)),
            pl.BlockSpec((None, PAST, W_AKV), lambda b, i: (b, 0, 0)),
        ],
        out_specs=pl.BlockSpec((QB, W_AQ), lambda b, i: (b * nb + i, 0)),
        compiler_params=_cparams(("parallel", "arbitrary")),
        name="attn_win",
    )(sink, qa, ka, va, ck, cv)


NB_HG = 12
NB_KEYS = NA_ROWS * GRID_W


def _row_start(r):
    return jnp.clip(r - NA_ROWS // 2, 0, GRID_H - NA_ROWS)


def _attn_nbr_kernel(q_ref, k_ref, v_ref, ck_ref, cv_ref, bias_ref, o_ref):
    r = pl.program_id(2)
    k0 = pl.multiple_of(_row_start(r) * GRID_W, GRID_W)
    q = q_ref[...]
    k = k_ref[pl.ds(k0, NB_KEYS), :]
    v = v_ref[pl.ds(k0, NB_KEYS), :]
    kk = jnp.concatenate([k, ck_ref[...]], axis=0)
    vv = jnp.concatenate([v, cv_ref[...]], axis=0)
    sls = [slice(h * HD, (h + 1) * HD) for h in range(NB_HG)]
    ss = [_dot_nt(q[:, sl], kk[:, sl]) * SCALE for sl in sls]
    ss = [jnp.concatenate([s[:, :NB_KEYS] + bias_ref[h], s[:, NB_KEYS:]], axis=1) for h, s in enumerate(ss)]
    ps = [_softmax_parts([s], None)[0] for s in ss]
    outs = [_dot(p, vv[:, sl]) for p, sl in zip(ps, sls)]
    o_ref[...] = jnp.concatenate(outs, axis=1).astype(BF)


def _attn_nbr(qb, kb, vb, ck, cv, bias):
    hw = NB_HG * HD
    n_hg = N_H_B // NB_HG
    return pl.pallas_call(
        _attn_nbr_kernel,
        out_shape=jax.ShapeDtypeStruct((TS, W_B), BF),
        grid=(DEC_BATCH, n_hg, GRID_H),
        in_specs=[
            pl.BlockSpec((GRID_W, hw), lambda b, g, r: (b * GRID_H + r, g)),
            pl.BlockSpec((DEC_SEQ, hw), lambda b, g, r: (b, g), pipeline_mode=pl.Buffered(1)),
            pl.BlockSpec((DEC_SEQ, hw), lambda b, g, r: (b, g), pipeline_mode=pl.Buffered(1)),
            pl.BlockSpec((None, PAST, hw), lambda b, g, r: (b, 0, g)),
            pl.BlockSpec((None, PAST, hw), lambda b, g, r: (b, 0, g)),
            pl.BlockSpec((None, NB_HG, GRID_W, NB_KEYS), lambda b, g, r: (r - _row_start(r), g, 0, 0)),
        ],
        out_specs=pl.BlockSpec((GRID_W, hw), lambda b, g, r: (b * GRID_H + r, g)),
        compiler_params=_cparams(("parallel", "parallel", "arbitrary")),
        name="attn_nbr",
    )(qb, kb, vb, ck, cv, bias)


def _rope_tables():
    pairs = HD // 4
    t = np.arange(DEC_SEQ)
    pos = np.stack([t // GRID_W, t % GRID_W], axis=-1).astype(np.float32)
    inv = jnp.asarray(ROPE_BASE, F32) ** (-jnp.arange(pairs, dtype=F32) * 2.0 / (HD // 2))
    ang = jnp.asarray(pos)[:, :, None] * inv
    cos = jnp.cos(ang)
    sin = jnp.sin(ang)
    cos_l = jnp.concatenate([cos, cos], axis=-1).reshape(DEC_SEQ, HD)
    sin_l = jnp.concatenate([-sin, sin], axis=-1).reshape(DEC_SEQ, HD)
    return cos_l, sin_l


def _nbr_bias_table(rpb):
    edge = GRID_W - NA_COLS
    ext = jnp.concatenate([jnp.repeat(rpb[..., :1], edge, axis=-1), rpb.astype(F32),
                           jnp.repeat(rpb[..., -1:], edge, axis=-1)], axis=-1)
    toep = jnp.stack([ext[..., GRID_W - 1 - q: 2 * GRID_W - 1 - q] for q in range(GRID_W)], axis=-2)
    qc = np.arange(GRID_W)
    cs = np.clip(qc - NA_COLS // 2, 0, GRID_W - NA_COLS)
    col_ok = (qc[None, :] >= cs[:, None]) & (qc[None, :] < cs[:, None] + NA_COLS)
    toep = jnp.where(jnp.asarray(col_ok), toep, NEG_INF)
    tabs = []
    for s in range(NA_ROWS):
        blk = toep[:, NA_ROWS - 1 - s: 2 * NA_ROWS - 1 - s]
        tabs.append(jnp.transpose(blk, (0, 2, 1, 3)).reshape(N_H_B, GRID_W, NB_KEYS))
    return jnp.stack(tabs, axis=0)


def kernel(x_prompt, x_sample, cache_a_k, cache_a_v, cache_b_k, cache_b_v, c, c_ctx,
           mod_w, mod_b, norm_ffn1, norm_mix, norm_ffn2, ffn1_w13, ffn1_w2, ffn2_w13, ffn2_w2,
           w_in, w_out, a_q_norm, a_k_norm, a_sink, b_q_norm, b_k_norm, b_rpb, conv_w, conv_b):
    hp = x_prompt.reshape(TP, D)
    hs = x_sample.reshape(TS, D)

    cvec = jnp.concatenate([c_ctx[None, :], c, jnp.zeros((MOD_ROWS - N_GROUPS, D), F32)], axis=0)
    mods = _modulation(cvec, mod_w, mod_b)
    mods = mods[:, :N_GROUPS].reshape(DEPTH * N_GROUPS * N_MOD, 1, D)

    cos_l, sin_l = _rope_tables()
    ffn1_w2p = _cast_pad_w2(ffn1_w2)
    ffn2_w2p = _cast_pad_w2(ffn2_w2)

    def ffn(h, latent, l, g, w13, w2p, k0):
        xn = _norm_mod(h, g[l][None, :], mods, l, k0, latent)
        return _ffn_down(_ffn_up(xn, w13, l), w2p, h, mods, l, k0 + 2, latent)

    new_ak, new_av, new_bk, new_bv = [], [], [], []
    for l in range(DEPTH):
        hp = ffn(hp, False, l, norm_ffn1, ffn1_w13, ffn1_w2p, 0)
        hs = ffn(hs, True, l, norm_ffn1, ffn1_w13, ffn1_w2p, 0)

        nrm = jnp.concatenate([a_q_norm[l][None], a_k_norm[l][None], b_q_norm[l][None], b_k_norm[l][None],
                               jnp.zeros((4, HD), F32)], axis=0)
        cw = jnp.concatenate([conv_w[l], jnp.zeros((5, CONV_CH), F32)], axis=0)
        cb = conv_b[l][None, :]
        sink = a_sink[l]
        g_mix = norm_mix[l][None, :]

        pp = _in_proj(_norm_mod(hp, g_mix, mods, l, 3, False), w_in, l)
        qa, ka, va, qb, kb, vb, cak, cav, cbk, cbv = _prep_ctx(pp, nrm)
        new_ak.append(cak.reshape(BATCH, SEQ, N_KV_A, HD))
        new_av.append(cav.reshape(BATCH, SEQ, N_KV_A, HD))
        new_bk.append(cbk.reshape(BATCH, SEQ, N_H_B, HD))
        new_bv.append(cbv.reshape(BATCH, SEQ, N_H_B, HD))
        o_ab = _attn_ctx(sink, qa, ka, va, qb, kb, vb)
        o_c = _short_conv(pp, cw, cb, SEQ, CONV_CH)
        hp = _out_proj([o_ab, o_c], w_out, hp, mods, l, False)

        ps = _in_proj(_norm_mod(hs, g_mix, mods, l, 3, True), w_in, l)
        qa, ka, va, qb, kb, vb = _prep_lat(ps, nrm, cos_l, sin_l)
        o_a = _attn_win(sink, qa, ka, va,
                        cache_a_k[:, l].reshape(DEC_BATCH, PAST, W_AKV).astype(BF),
                        cache_a_v[:, l].reshape(DEC_BATCH, PAST, W_AKV).astype(BF))
        o_b = _attn_nbr(qb, kb, vb,
                        cache_b_k[:, l].reshape(DEC_BATCH, PAST, W_B).astype(BF),
                        cache_b_v[:, l].reshape(DEC_BATCH, PAST, W_B).astype(BF),
                        _nbr_bias_table(b_rpb[l]))
        o_c = _short_conv(ps, cw, cb, DEC_SEQ, 128)
        hs = _out_proj([o_a, o_b, o_c], w_out, hs, mods, l, True)

        hp = ffn(hp, False, l, norm_ffn2, ffn2_w13, ffn2_w2p, 6)
        hs = ffn(hs, True, l, norm_ffn2, ffn2_w13, ffn2_w2p, 6)

    return (hp.reshape(BATCH, SEQ, D), hs.reshape(DEC_BATCH, DEC_SEQ, D),
            jnp.stack(new_ak, axis=1), jnp.stack(new_av, axis=1),
            jnp.stack(new_bk, axis=1), jnp.stack(new_bv, axis=1))
```

```python
import functools

import jax
import jax.numpy as jnp
import numpy as np
from jax import lax
from jax.experimental import pallas as pl
from jax.experimental.pallas import tpu as pltpu

D = 4096
BATCH = 16
SEQ = 256
DEPTH = 2
DEC_BATCH = 2
DEC_SEQ = 4096
PAST = 512
GRID_W = 64
GRID_H = DEC_SEQ // GRID_W
HD = 128
N_Q_A = 12
N_KV_A = 4
GQA = N_Q_A // N_KV_A
WINDOW = 128
N_H_B = 12
NA_ROWS = 8
NA_COLS = 16
CONV_CH = 1024
D_FF = 11008
N_MOD = 9
ROPE_BASE = 10000.0
EPS = 1e-6
NEG_INF = -1e30
SCALE = HD ** -0.5

TP = BATCH * SEQ
TS = DEC_BATCH * DEC_SEQ
N_GROUPS = 1 + DEC_BATCH
W_AQ = N_Q_A * HD
W_AKV = N_KV_A * HD
W_B = N_H_B * HD
QKV_W = W_AQ + 2 * W_AKV + 3 * W_B
IN_W = QKV_W + 3 * CONV_CH
MIX_W = W_AQ + W_B + CONV_CH

FF_BLK = 256
N_FF_BLK = D_FF // FF_BLK
FF_PAD = (N_FF_BLK + 1) * FF_BLK

VMEM_LIMIT = 56 * 1024 * 1024

BF = jnp.bfloat16
F32 = jnp.float32


def _cparams(sem, vmem=VMEM_LIMIT):
    return pltpu.CompilerParams(dimension_semantics=sem, vmem_limit_bytes=vmem)


def _dot(a, b):
    return jnp.dot(a, b, preferred_element_type=F32)


def _dot_nt(a, b):
    return lax.dot_general(a, b, (((1,), (1,)), ((), ())), preferred_element_type=F32)


def _mod_index(layer, k, rows_per_block, latent):
    base = layer * N_GROUPS * N_MOD + k
    if not latent:
        return lambda i: base
    blocks_per_group = DEC_SEQ // rows_per_block
    return lambda i: base + (1 + i // blocks_per_group) * N_MOD


MOD_TN = 512
MOD_ROWS = 16


def _mod_kernel(c_ref, w_ref, b_ref, o_ref):
    c = c_ref[...]
    x = (c * jax.nn.sigmoid(c)).astype(BF)
    o_ref[...] = _dot(x, w_ref[...].astype(BF)) + b_ref[...]


def _modulation(cvec, mod_w, mod_b):
    n = N_MOD * D
    return pl.pallas_call(
        _mod_kernel,
        out_shape=jax.ShapeDtypeStruct((DEPTH, MOD_ROWS, n), F32),
        grid=(DEPTH, n // MOD_TN),
        in_specs=[
            pl.BlockSpec((MOD_ROWS, D), lambda l, j: (0, 0)),
            pl.BlockSpec((None, D, MOD_TN), lambda l, j: (l, 0, j)),
            pl.BlockSpec((None, 1, MOD_TN), lambda l, j: (l, 0, j)),
        ],
        out_specs=pl.BlockSpec((None, MOD_ROWS, MOD_TN), lambda l, j: (l, 0, j)),
        compiler_params=_cparams(("parallel", "parallel")),
        name="modulation",
    )(cvec, mod_w, mod_b.reshape(DEPTH, 1, n))


NORM_TM = 512


def _norm_kernel(h_ref, g_ref, shift_ref, scale_ref, o_ref):
    x = h_ref[...]
    ms = jnp.mean(x * x, axis=-1, keepdims=True)
    y = x * lax.rsqrt(ms + EPS) * g_ref[...]
    o_ref[...] = (y * (1.0 + scale_ref[...]) + shift_ref[...]).astype(BF)


def _norm_mod(h, g, mods, layer, k_shift, latent):
    rows = h.shape[0]
    ish = _mod_index(layer, k_shift, NORM_TM, latent)
    isc = _mod_index(layer, k_shift + 1, NORM_TM, latent)
    return pl.pallas_call(
        _norm_kernel,
        out_shape=jax.ShapeDtypeStruct((rows, D), BF),
        grid=(rows // NORM_TM,),
        in_specs=[
            pl.BlockSpec((NORM_TM, D), lambda i: (i, 0)),
            pl.BlockSpec((1, D), lambda i: (0, 0)),
            pl.BlockSpec((None, 1, D), lambda i: (ish(i), 0, 0)),
            pl.BlockSpec((None, 1, D), lambda i: (isc(i), 0, 0)),
        ],
        out_specs=pl.BlockSpec((NORM_TM, D), lambda i: (i, 0)),
        compiler_params=_cparams(("parallel",)),
        name="norm_mod",
    )(h, g, mods, mods)


MM_TM = 2048


def _resident_lhs_spec(tm, width):
    return pl.BlockSpec((tm, width), lambda i, j: (i, 0), pipeline_mode=pl.Buffered(1))


def _ffn_up_kernel(x_ref, w1_ref, w3_ref, *rest, w2_slabs):
    if w2_slabs:
        w2_ref, o_ref, w2p_ref = rest
    else:
        (o_ref,) = rest
    j = pl.program_id(1)

    @pl.when(j < N_FF_BLK)
    def _():
        x = x_ref[...]
        a = _dot(x, w1_ref[...].astype(BF))
        b = _dot(x, w3_ref[...].astype(BF))
        o_ref[...] = (a * jax.nn.sigmoid(a) * b).astype(BF)

    @pl.when(j == N_FF_BLK)
    def _():
        o_ref[...] = jnp.zeros_like(o_ref)

    if w2_slabs:
        s = pl.program_id(0) * pl.num_programs(1) + j

        @pl.when(s < w2_slabs)
        def _():
            w2p_ref[...] = w2_ref[...].astype(BF)

        @pl.when(s >= w2_slabs)
        def _():
            w2p_ref[...] = jnp.zeros_like(w2p_ref)


def _ffn_up(xn, w13, layer, w2=None):
    rows = xn.shape[0]
    last = N_FF_BLK - 1
    n_j = N_FF_BLK + 1
    in_specs = [
        _resident_lhs_spec(MM_TM, D),
        pl.BlockSpec((None, D, FF_BLK), lambda i, j: (layer, 0, jnp.minimum(j, last))),
        pl.BlockSpec((None, D, FF_BLK), lambda i, j: (layer, 0, N_FF_BLK + jnp.minimum(j, last))),
    ]
    out_shape = [jax.ShapeDtypeStruct((rows, FF_PAD), BF)]
    out_specs = [pl.BlockSpec((MM_TM, FF_BLK), lambda i, j: (i, j))]
    args = [xn, w13, w13]
    w2_slabs = 0
    if w2 is not None:
        n_steps = (rows // MM_TM) * n_j
        assert FF_PAD % n_steps == 0
        slab = FF_PAD // n_steps
        assert D_FF % slab == 0 and slab % 16 == 0
        w2_slabs = D_FF // slab
        in_specs.append(pl.BlockSpec((None, slab, D), lambda i, j: (layer, jnp.minimum(i * n_j + j, w2_slabs - 1), 0)))
        out_shape.append(jax.ShapeDtypeStruct((FF_PAD, D), BF))
        out_specs.append(pl.BlockSpec((slab, D), lambda i, j: (i * n_j + j, 0)))
        args.append(w2)
    outs = pl.pallas_call(
        functools.partial(_ffn_up_kernel, w2_slabs=w2_slabs),
        out_shape=out_shape,
        grid=(rows // MM_TM, n_j),
        in_specs=in_specs,
        out_specs=out_specs,
        compiler_params=_cparams(("arbitrary", "arbitrary")),
        name="ffn_up",
    )(*args)
    return outs if w2 is not None else outs[0]


DOWN_TM = 1024
DOWN_TN = 1024
DOWN_TK = FF_PAD // 4


def _ffn_down_kernel(x_ref, w_ref, h_ref, gate_ref, o_ref):
    k = pl.program_id(2)
    last = pl.num_programs(2) - 1

    @pl.when(k == 0)
    def _():
        o_ref[...] = _dot(x_ref[...], w_ref[...])

    @pl.when(jnp.logical_and(k > 0, k < last))
    def _():
        o_ref[...] += _dot(x_ref[...], w_ref[...])

    @pl.when(k == last)
    def _():
        acc = o_ref[...] + _dot(x_ref[...], w_ref[...])
        o_ref[...] = h_ref[...] + (0.5 * gate_ref[...]) * acc


def _ffn_down(x, w2p, h, mods, layer, k_gate, latent):
    rows = x.shape[0]
    ig = _mod_index(layer, k_gate, DOWN_TM, latent)
    return pl.pallas_call(
        _ffn_down_kernel,
        out_shape=jax.ShapeDtypeStruct((rows, D), F32),
        grid=(rows // DOWN_TM, D // DOWN_TN, FF_PAD // DOWN_TK),
        in_specs=[
            pl.BlockSpec((DOWN_TM, DOWN_TK), lambda i, j, k: (i, k)),
            pl.BlockSpec((DOWN_TK, DOWN_TN), lambda i, j, k: (k, j)),
            pl.BlockSpec((DOWN_TM, DOWN_TN), lambda i, j, k: (i, j)),
            pl.BlockSpec((None, 1, DOWN_TN), lambda i, j, k: (ig(i), 0, j)),
        ],
        out_specs=pl.BlockSpec((DOWN_TM, DOWN_TN), lambda i, j, k: (i, j)),
        compiler_params=_cparams(("parallel", "parallel", "arbitrary")),
        name="ffn_down",
    )(x, w2p, h, mods)


IN_TN = 512


def _in_proj_kernel(x_ref, w_ref, o_ref):
    o_ref[...] = _dot(x_ref[...], w_ref[...].astype(BF))


def _in_proj(u, w_in, layer):
    rows = u.shape[0]
    return pl.pallas_call(
        _in_proj_kernel,
        out_shape=jax.ShapeDtypeStruct((rows, IN_W), F32),
        grid=(rows // MM_TM, IN_W // IN_TN),
        in_specs=[
            _resident_lhs_spec(MM_TM, D),
            pl.BlockSpec((None, D, IN_TN), lambda i, j: (layer, 0, j)),
        ],
        out_specs=pl.BlockSpec((MM_TM, IN_TN), lambda i, j: (i, j)),
        compiler_params=_cparams(("parallel", "arbitrary")),
        name="in_proj",
    )(u, w_in)


OUT_TN = 512


def _out_proj_kernel(*refs, n_lhs):
    xs, ws = refs[:n_lhs], refs[n_lhs:2 * n_lhs]
    h_ref, gate_ref, o_ref = refs[2 * n_lhs:]
    acc = _dot(xs[0][...], ws[0][...].astype(BF))
    for x_ref, w_ref in zip(xs[1:], ws[1:]):
        acc += _dot(x_ref[...], w_ref[...].astype(BF))
    o_ref[...] = h_ref[...] + gate_ref[...] * acc


def _out_proj(xs, w_out, h, mods, layer, latent):
    rows = h.shape[0]
    ig = _mod_index(layer, 5, MM_TM, latent)
    x_specs, w_specs, c0 = [], [], 0
    for x in xs:
        w = x.shape[1]
        assert c0 % w == 0
        x_specs.append(_resident_lhs_spec(MM_TM, w))
        w_specs.append(pl.BlockSpec((None, w, OUT_TN), functools.partial(lambda i, j, r: (layer, r, j), r=c0 // w)))
        c0 += w
    assert c0 == MIX_W
    return pl.pallas_call(
        functools.partial(_out_proj_kernel, n_lhs=len(xs)),
        out_shape=jax.ShapeDtypeStruct((rows, D), F32),
        grid=(rows // MM_TM, D // OUT_TN),
        in_specs=x_specs + w_specs + [
            pl.BlockSpec((MM_TM, OUT_TN), lambda i, j: (i, j)),
            pl.BlockSpec((None, 1, OUT_TN), lambda i, j: (ig(i), 0, j)),
        ],
        out_specs=pl.BlockSpec((MM_TM, OUT_TN), lambda i, j: (i, j)),
        compiler_params=_cparams(("parallel", "arbitrary")),
        name="out_proj",
    )(*xs, *([w_out] * len(xs)), h, mods)


PREP_TM = 256
C_QA = 0
C_KA = C_QA + W_AQ
C_VA = C_KA + W_AKV
C_QB = C_VA + W_AKV
C_KB = C_QB + W_B
C_VB = C_KB + W_B


def _head_rms(x, g):
    ms = jnp.mean(x * x, axis=-1, keepdims=True)
    return x * lax.rsqrt(ms + EPS) * g


def _rope(x, cos, sin_signed, lo_lane):
    swapped = jnp.where(lo_lane, pltpu.roll(x, HD - 32, axis=1), pltpu.roll(x, 32, axis=1))
    return x * cos + swapped * sin_signed


def _prep_body(p_ref, nrm_ref, qa_ref, ka_ref, va_ref, qb_ref, kb_ref, vb_ref, rope, caches):
    g_aq = nrm_ref[0:1, :]
    g_ak = nrm_ref[1:2, :]
    g_bq = nrm_ref[2:3, :]
    g_bk = nrm_ref[3:4, :]
    if rope is not None:
        cos, sin_signed = rope
        lo_lane = (lax.broadcasted_iota(jnp.int32, (PREP_TM, HD), 1) % 64) < 32

    def col(c0, h):
        return p_ref[:, c0 + h * HD: c0 + (h + 1) * HD]

    for h in range(N_Q_A):
        y = _head_rms(col(C_QA, h), g_aq)
        if rope is not None:
            y = _rope(y, cos, sin_signed, lo_lane)
        qa_ref[:, h * HD:(h + 1) * HD] = y.astype(BF)
    for h in range(N_KV_A):
        y = _head_rms(col(C_KA, h), g_ak)
        if caches is not None:
            caches[0][:, h * HD:(h + 1) * HD] = y
        if rope is not None:
            y = _rope(y, cos, sin_signed, lo_lane)
        ka_ref[:, h * HD:(h + 1) * HD] = y.astype(BF)
    va = p_ref[:, C_VA:C_VA + W_AKV]
    va_ref[...] = va.astype(BF)
    if caches is not None:
        caches[1][...] = va
    for h in range(N_H_B):
        qb_ref[:, h * HD:(h + 1) * HD] = _head_rms(col(C_QB, h), g_bq).astype(BF)
        y = _head_rms(col(C_KB, h), g_bk)
        if caches is not None:
            caches[2][:, h * HD:(h + 1) * HD] = y
        kb_ref[:, h * HD:(h + 1) * HD] = y.astype(BF)
    vb = p_ref[:, C_VB:C_VB + W_B]
    vb_ref[...] = vb.astype(BF)
    if caches is not None:
        caches[3][...] = vb


def _prep_ctx_kernel(p_ref, nrm_ref, qa_ref, ka_ref, va_ref, qb_ref, kb_ref, vb_ref,
                     cak_ref, cav_ref, cbk_ref, cbv_ref):
    _prep_body(p_ref, nrm_ref, qa_ref, ka_ref, va_ref, qb_ref, kb_ref, vb_ref,
               None, (cak_ref, cav_ref, cbk_ref, cbv_ref))


def _prep_lat_kernel(p_ref, nrm_ref, cos_ref, sin_ref, qa_ref, ka_ref, va_ref, qb_ref, kb_ref, vb_ref):
    _prep_body(p_ref, nrm_ref, qa_ref, ka_ref, va_ref, qb_ref, kb_ref, vb_ref,
               (cos_ref[...], sin_ref[...]), None)


def _qkv_specs(rows):
    widths = (W_AQ, W_AKV, W_AKV, W_B, W_B, W_B)
    shapes = [jax.ShapeDtypeStruct((rows, w), BF) for w in widths]
    specs = [pl.BlockSpec((PREP_TM, w), lambda i: (i, 0)) for w in widths]
    return shapes, specs


def _prep_ctx(p, nrm):
    shapes, specs = _qkv_specs(TP)
    cache_w = (W_AKV, W_AKV, W_B, W_B)
    shapes += [jax.ShapeDtypeStruct((TP, w), F32) for w in cache_w]
    specs += [pl.BlockSpec((PREP_TM, w), lambda i: (i, 0)) for w in cache_w]
    return pl.pallas_call(
        _prep_ctx_kernel,
        out_shape=shapes,
        grid=(TP // PREP_TM,),
        in_specs=[
            pl.BlockSpec((PREP_TM, QKV_W), lambda i: (i, 0)),
            pl.BlockSpec((8, HD), lambda i: (0, 0)),
        ],
        out_specs=specs,
        compiler_params=_cparams(("parallel",)),
        name="prep_ctx",
    )(p, nrm)


def _prep_lat(p, nrm, cos, sin_signed):
    shapes, specs = _qkv_specs(TS)
    per_seq = DEC_SEQ // PREP_TM
    return pl.pallas_call(
        _prep_lat_kernel,
        out_shape=shapes,
        grid=(TS // PREP_TM,),
        in_specs=[
            pl.BlockSpec((PREP_TM, QKV_W), lambda i: (i, 0)),
            pl.BlockSpec((8, HD), lambda i: (0, 0)),
            pl.BlockSpec((PREP_TM, HD), lambda i: (i % per_seq, 0)),
            pl.BlockSpec((PREP_TM, HD), lambda i: (i % per_seq, 0)),
        ],
        out_specs=specs,
        compiler_params=_cparams(("parallel",)),
        name="prep_lat",
    )(p, nrm, cos, sin_signed)


def _conv_kernel(xc_ref, gb_ref, gc_ref, w_ref, b_ref, o_ref):
    u = gc_ref[...] * xc_ref[...]
    n = u.shape[0]
    row = lax.broadcasted_iota(jnp.int32, u.shape, 0)
    prev = jnp.where(row == 0, 0.0, pltpu.roll(u, 1, axis=0))
    nxt = jnp.where(row == n - 1, 0.0, pltpu.roll(u, n - 1, axis=0))
    y = w_ref[0:1, :] * prev + w_ref[1:2, :] * u + w_ref[2:3, :] * nxt + b_ref[...]
    o_ref[...] = (gb_ref[...] * y).astype(BF)


def _short_conv(p, conv_w, conv_b, rows, ch):
    n_seq = p.shape[0] // rows
    c0 = QKV_W // ch
    per = CONV_CH // ch
    return pl.pallas_call(
        _conv_kernel,
        out_shape=jax.ShapeDtypeStruct((n_seq * rows, CONV_CH), BF),
        grid=(n_seq, per),
        in_specs=[
            pl.BlockSpec((rows, ch), lambda s, c: (s, c0 + c)),
            pl.BlockSpec((rows, ch), lambda s, c: (s, c0 + per + c)),
            pl.BlockSpec((rows, ch), lambda s, c: (s, c0 + 2 * per + c)),
            pl.BlockSpec((8, ch), lambda s, c: (0, c)),
            pl.BlockSpec((1, ch), lambda s, c: (0, c)),
        ],
        out_specs=pl.BlockSpec((rows, ch), lambda s, c: (s, c)),
        compiler_params=_cparams(("parallel", "parallel")),
        name="short_conv",
    )(p, p, p, conv_w, conv_b)


def _softmax_parts(parts, sink):
    m = parts[0].max(axis=-1, keepdims=True)
    for s in parts[1:]:
        m = jnp.maximum(m, s.max(axis=-1, keepdims=True))
    if sink is not None:
        m = jnp.maximum(m, sink)
    es = [jnp.exp(s - m) for s in parts]
    den = es[0].sum(axis=-1, keepdims=True)
    for e in es[1:]:
        den = den + e.sum(axis=-1, keepdims=True)
    if sink is not None:
        den = den + jnp.exp(sink - m)
    inv = 1.0 / den
    return [(e * inv).astype(BF) for e in es]


def _sink_column(sink_ref, kv, rows):
    return jnp.concatenate(
        [jnp.full((rows, 1), sink_ref[kv * GQA + g], F32) for g in range(GQA)], axis=0)


def _group_rows(x, kv):
    return jnp.concatenate([x[:, (kv * GQA + g) * HD:(kv * GQA + g + 1) * HD] for g in range(GQA)], axis=0)


def _ungroup_rows(o, rows):
    return [o[g * rows:(g + 1) * rows] for g in range(GQA)]


def _attn_ctx_kernel(sink_ref, qa_ref, ka_ref, va_ref, qb_ref, kb_ref, vb_ref, o_ref):
    qa, ka, va = qa_ref[...], ka_ref[...], va_ref[...]
    qb, kb, vb = qb_ref[...], kb_ref[...], vb_ref[...]
    sls = [slice(h * HD, (h + 1) * HD) for h in range(N_H_B)]
    s_a = [_dot_nt(_group_rows(qa, kv), ka[:, sls[kv]]) * SCALE for kv in range(N_KV_A)]
    s_b = [_dot_nt(qb[:, sl], kb[:, sl]) * SCALE for sl in sls]
    p_a = [_softmax_parts([s], _sink_column(sink_ref, kv, SEQ))[0] for kv, s in enumerate(s_a)]
    p_b = [_softmax_parts([s], None)[0] for s in s_b]
    outs = []
    for kv, p in enumerate(p_a):
        outs += _ungroup_rows(_dot(p, va[:, sls[kv]]), SEQ)
    outs += [_dot(p, vb[:, sl]) for p, sl in zip(p_b, sls)]
    o_ref[...] = jnp.concatenate(outs, axis=1).astype(BF)


def _attn_ctx(sink, qa, ka, va, qb, kb, vb):
    def spec(w):
        return pl.BlockSpec((SEQ, w), lambda b: (b, 0))
    return pl.pallas_call(
        _attn_ctx_kernel,
        out_shape=jax.ShapeDtypeStruct((TP, W_AQ + W_B), BF),
        grid=(BATCH,),
        in_specs=[pl.BlockSpec(memory_space=pltpu.SMEM),
                  spec(W_AQ), spec(W_AKV), spec(W_AKV), spec(W_B), spec(W_B), spec(W_B)],
        out_specs=spec(W_AQ + W_B),
        compiler_params=_cparams(("parallel",)),
        name="attn_ctx",
    )(sink, qa, ka, va, qb, kb, vb)


QB = 128
SPAN = QB + 2 * WINDOW


def _attn_win_kernel(sink_ref, q_ref, k_ref, v_ref, ck_ref, cv_ref, o_ref):
    i = pl.program_id(1)
    start = i * QB
    s0 = pl.multiple_of(jnp.clip(start - WINDOW, 0, DEC_SEQ - SPAN), QB)
    qpos = start + lax.broadcasted_iota(jnp.int32, (GQA * QB, SPAN), 0) % QB
    kpos = s0 + lax.broadcasted_iota(jnp.int32, (GQA * QB, SPAN), 1)
    valid = jnp.abs(qpos - kpos) <= WINDOW
    q = q_ref[...]
    k = k_ref[pl.ds(s0, SPAN), :]
    v = v_ref[pl.ds(s0, SPAN), :]
    kk = jnp.concatenate([k, ck_ref[...]], axis=0)
    vv = jnp.concatenate([v, cv_ref[...]], axis=0)
    sls = [slice(kv * HD, (kv + 1) * HD) for kv in range(N_KV_A)]
    ss = [_dot_nt(_group_rows(q, kv), kk[:, sl]) * SCALE for kv, sl in enumerate(sls)]
    ss = [jnp.concatenate([jnp.where(valid, s[:, :SPAN], NEG_INF), s[:, SPAN:]], axis=1) for s in ss]
    ps = [_softmax_parts([s], _sink_column(sink_ref, kv, QB))[0] for kv, s in enumerate(ss)]
    outs = []
    for p, sl in zip(ps, sls):
        outs += _ungroup_rows(_dot(p, vv[:, sl]), QB)
    o_ref[...] = jnp.concatenate(outs, axis=1).astype(BF)


def _attn_win(sink, qa, ka, va, ck, cv):
    nb = DEC_SEQ // QB
    return pl.pallas_call(
        _attn_win_kernel,
        out_shape=jax.ShapeDtypeStruct((TS, W_AQ), BF),
        grid=(DEC_BATCH, nb),
        in_specs=[
            pl.BlockSpec(memory_space=pltpu.SMEM),
            pl.BlockSpec((QB, W_AQ), lambda b, i: (b * nb + i, 0)),
            pl.BlockSpec((DEC_SEQ, W_AKV), lambda b, i: (b, 0)),
            pl.BlockSpec((DEC_SEQ, W_AKV), lambda b, i: (b, 0)),
            pl.BlockSpec((None, PAST, W_AKV), lambda b, i: (b, 0, 0)),
            pl.BlockSpec((None, PAST, W_AKV), lambda b, i: (b, 0, 0)),
        ],
        out_specs=pl.BlockSpec((QB, W_AQ), lambda b, i: (b * nb + i, 0)),
        compiler_params=_cparams(("parallel", "arbitrary")),
        name="attn_win",
    )(sink, qa, ka, va, ck, cv)


NB_HG = 12
NB_KEYS = NA_ROWS * GRID_W


def _row_start(r):
    return jnp.clip(r - NA_ROWS // 2, 0, GRID_H - NA_ROWS)


def _attn_nbr_kernel(q_ref, k_ref, v_ref, ck_ref, cv_ref, bias_ref, o_ref):
    r = pl.program_id(2)
    k0 = pl.multiple_of(_row_start(r) * GRID_W, GRID_W)
    q = q_ref[...]
    k = k_ref[pl.ds(k0, NB_KEYS), :]
    v = v_ref[pl.ds(k0, NB_KEYS), :]
    kk = jnp.concatenate([k, ck_ref[...]], axis=0)
    vv = jnp.concatenate([v, cv_ref[...]], axis=0)
    sls = [slice(h * HD, (h + 1) * HD) for h in range(NB_HG)]
    ss = [_dot_nt(q[:, sl], kk[:, sl]) * SCALE for sl in sls]
    ss = [jnp.concatenate([s[:, :NB_KEYS] + bias_ref[h], s[:, NB_KEYS:]], axis=1) for h, s in enumerate(ss)]
    ps = [_softmax_parts([s], None)[0] for s in ss]
    outs = [_dot(p, vv[:, sl]) for p, sl in zip(ps, sls)]
    o_ref[...] = jnp.concatenate(outs, axis=1).astype(BF)


def _attn_nbr(qb, kb, vb, ck, cv, bias):
    hw = NB_HG * HD
    n_hg = N_H_B // NB_HG
    return pl.pallas_call(
        _attn_nbr_kernel,
        out_shape=jax.ShapeDtypeStruct((TS, W_B), BF),
        grid=(DEC_BATCH, n_hg, GRID_H),
        in_specs=[
            pl.BlockSpec((GRID_W, hw), lambda b, g, r: (b * GRID_H + r, g)),
            pl.BlockSpec((DEC_SEQ, hw), lambda b, g, r: (b, g), pipeline_mode=pl.Buffered(1)),
            pl.BlockSpec((DEC_SEQ, hw), lambda b, g, r: (b, g), pipeline_mode=pl.Buffered(1)),
            pl.BlockSpec((None, PAST, hw), lambda b, g, r: (b, 0, g)),
            pl.BlockSpec((None, PAST, hw), lambda b, g, r: (b, 0, g)),
            pl.BlockSpec((None, NB_HG, GRID_W, NB_KEYS), lambda b, g, r: (r - _row_start(r), g, 0, 0)),
        ],
        out_specs=pl.BlockSpec((GRID_W, hw), lambda b, g, r: (b * GRID_H + r, g)),
        compiler_params=_cparams(("parallel", "parallel", "arbitrary")),
        name="attn_nbr",
    )(qb, kb, vb, ck, cv, bias)


def _rope_tables():
    pairs = HD // 4
    t = np.arange(DEC_SEQ)
    pos = np.stack([t // GRID_W, t % GRID_W], axis=-1).astype(np.float32)
    inv = jnp.asarray(ROPE_BASE, F32) ** (-jnp.arange(pairs, dtype=F32) * 2.0 / (HD // 2))
    ang = jnp.asarray(pos)[:, :, None] * inv
    cos = jnp.cos(ang)
    sin = jnp.sin(ang)
    cos_l = jnp.concatenate([cos, cos], axis=-1).reshape(DEC_SEQ, HD)
    sin_l = jnp.concatenate([-sin, sin], axis=-1).reshape(DEC_SEQ, HD)
    return cos_l, sin_l


def _nbr_bias_table(rpb):
    edge = GRID_W - NA_COLS
    ext = jnp.concatenate([jnp.repeat(rpb[..., :1], edge, axis=-1), rpb.astype(F32),
                           jnp.repeat(rpb[..., -1:], edge, axis=-1)], axis=-1)
    toep = jnp.stack([ext[..., GRID_W - 1 - q: 2 * GRID_W - 1 - q] for q in range(GRID_W)], axis=-2)
    qc = np.arange(GRID_W)
    cs = np.clip(qc - NA_COLS // 2, 0, GRID_W - NA_COLS)
    col_ok = (qc[None, :] >= cs[:, None]) & (qc[None, :] < cs[:, None] + NA_COLS)
    toep = jnp.where(jnp.asarray(col_ok), toep, NEG_INF)
    tabs = []
    for s in range(NA_ROWS):
        blk = toep[:, NA_ROWS - 1 - s: 2 * NA_ROWS - 1 - s]
        tabs.append(jnp.transpose(blk, (0, 2, 1, 3)).reshape(N_H_B, GRID_W, NB_KEYS))
    return jnp.stack(tabs, axis=0)


def kernel(x_prompt, x_sample, cache_a_k, cache_a_v, cache_b_k, cache_b_v, c, c_ctx,
           mod_w, mod_b, norm_ffn1, norm_mix, norm_ffn2, ffn1_w13, ffn1_w2, ffn2_w13, ffn2_w2,
           w_in, w_out, a_q_norm, a_k_norm, a_sink, b_q_norm, b_k_norm, b_rpb, conv_w, conv_b):
    hp = x_prompt.reshape(TP, D)
    hs = x_sample.reshape(TS, D)

    cvec = jnp.concatenate([c_ctx[None, :], c, jnp.zeros((MOD_ROWS - N_GROUPS, D), F32)], axis=0)
    mods = _modulation(cvec, mod_w, mod_b)
    mods = mods[:, :N_GROUPS].reshape(DEPTH * N_GROUPS * N_MOD, 1, D)

    cos_l, sin_l = _rope_tables()

    def ffn_pair(hp, hs, l, g, w13, w2, k0):
        g = g[l][None, :]
        hid, w2p = _ffn_up(_norm_mod(hp, g, mods, l, k0, False), w13, l, w2)
        hp = _ffn_down(hid, w2p, hp, mods, l, k0 + 2, False)
        hid = _ffn_up(_norm_mod(hs, g, mods, l, k0, True), w13, l)
        hs = _ffn_down(hid, w2p, hs, mods, l, k0 + 2, True)
        return hp, hs

    new_ak, new_av, new_bk, new_bv = [], [], [], []
    for l in range(DEPTH):
        hp, hs = ffn_pair(hp, hs, l, norm_ffn1, ffn1_w13, ffn1_w2, 0)

        nrm = jnp.concatenate([a_q_norm[l][None], a_k_norm[l][None], b_q_norm[l][None], b_k_norm[l][None],
                               jnp.zeros((4, HD), F32)], axis=0)
        cw = jnp.concatenate([conv_w[l], jnp.zeros((5, CONV_CH), F32)], axis=0)
        cb = conv_b[l][None, :]
        sink = a_sink[l]
        g_mix = norm_mix[l][None, :]

        pp = _in_proj(_norm_mod(hp, g_mix, mods, l, 3, False), w_in, l)
        qa, ka, va, qb, kb, vb, cak, cav, cbk, cbv = _prep_ctx(pp, nrm)
        new_ak.append(cak.reshape(BATCH, SEQ, N_KV_A, HD))
        new_av.append(cav.reshape(BATCH, SEQ, N_KV_A, HD))
        new_bk.append(cbk.reshape(BATCH, SEQ, N_H_B, HD))
        new_bv.append(cbv.reshape(BATCH, SEQ, N_H_B, HD))
        o_ab = _attn_ctx(sink, qa, ka, va, qb, kb, vb)
        o_c = _short_conv(pp, cw, cb, SEQ, CONV_CH)
        hp = _out_proj([o_ab, o_c], w_out, hp, mods, l, False)

        ps = _in_proj(_norm_mod(hs, g_mix, mods, l, 3, True), w_in, l)
        qa, ka, va, qb, kb, vb = _prep_lat(ps, nrm, cos_l, sin_l)
        o_a = _attn_win(sink, qa, ka, va,
                        cache_a_k[:, l].reshape(DEC_BATCH, PAST, W_AKV).astype(BF),
                        cache_a_v[:, l].reshape(DEC_BATCH, PAST, W_AKV).astype(BF))
        o_b = _attn_nbr(qb, kb, vb,
                        cache_b_k[:, l].reshape(DEC_BATCH, PAST, W_B).astype(BF),
                        cache_b_v[:, l].reshape(DEC_BATCH, PAST, W_B).astype(BF),
                        _nbr_bias_table(b_rpb[l]))
        o_c = _short_conv(ps, cw, cb, DEC_SEQ, 128)
        hs = _out_proj([o_a, o_b, o_c], w_out, hs, mods, l, True)

        hp, hs = ffn_pair(hp, hs, l, norm_ffn2, ffn2_w13, ffn2_w2, 6)

    return (hp.reshape(BATCH, SEQ, D), hs.reshape(DEC_BATCH, DEC_SEQ, D),
            jnp.stack(new_ak, axis=1), jnp.stack(new_av, axis=1),
            jnp.stack(new_bk, axis=1), jnp.stack(new_bv, axis=1))
```

```python
import functools

import jax
import jax.numpy as jnp
import numpy as np
from jax import lax
from jax.experimental import pallas as pl
from jax.experimental.pallas import tpu as pltpu

D = 4096
BATCH = 16
SEQ = 256
DEPTH = 2
DEC_BATCH = 2
DEC_SEQ = 4096
PAST = 512
GRID_W = 64
GRID_H = DEC_SEQ // GRID_W
HD = 128
N_Q_A = 12
N_KV_A = 4
GQA = N_Q_A // N_KV_A
WINDOW = 128
N_H_B = 12
NA_ROWS = 8
NA_COLS = 16
CONV_CH = 1024
D_FF = 11008
N_MOD = 9
ROPE_BASE = 10000.0
EPS = 1e-6
NEG_INF = -1e30
SCALE = HD ** -0.5

TP = BATCH * SEQ
TS = DEC_BATCH * DEC_SEQ
N_GROUPS = 1 + DEC_BATCH
W_AQ = N_Q_A * HD
W_AKV = N_KV_A * HD
W_B = N_H_B * HD
QKV_W = W_AQ + 2 * W_AKV + 3 * W_B
IN_W = QKV_W + 3 * CONV_CH
MIX_W = W_AQ + W_B + CONV_CH

FF_BLK = 256
N_FF_BLK = D_FF // FF_BLK
FF_PAD = (N_FF_BLK + 1) * FF_BLK

VMEM_LIMIT = 56 * 1024 * 1024

BF = jnp.bfloat16
F32 = jnp.float32


def _cparams(sem, vmem=VMEM_LIMIT):
    return pltpu.CompilerParams(dimension_semantics=sem, vmem_limit_bytes=vmem)


def _dot(a, b):
    return jnp.dot(a, b, preferred_element_type=F32)


def _dot_nt(a, b):
    return lax.dot_general(a, b, (((1,), (1,)), ((), ())), preferred_element_type=F32)


def _mod_index(layer, k, rows_per_block, latent):
    base = layer * N_GROUPS * N_MOD + k
    if not latent:
        return lambda i: base
    blocks_per_group = DEC_SEQ // rows_per_block
    return lambda i: base + (1 + i // blocks_per_group) * N_MOD


MOD_TN = 512
MOD_ROWS = 16


def _mod_kernel(c_ref, w_ref, b_ref, o_ref):
    c = c_ref[...]
    x = (c * jax.nn.sigmoid(c)).astype(BF)
    o_ref[...] = _dot(x, w_ref[...].astype(BF)) + b_ref[...]


def _modulation(cvec, mod_w, mod_b):
    n = N_MOD * D
    return pl.pallas_call(
        _mod_kernel,
        out_shape=jax.ShapeDtypeStruct((DEPTH, MOD_ROWS, n), F32),
        grid=(DEPTH, n // MOD_TN),
        in_specs=[
            pl.BlockSpec((MOD_ROWS, D), lambda l, j: (0, 0)),
            pl.BlockSpec((None, D, MOD_TN), lambda l, j: (l, 0, j)),
            pl.BlockSpec((None, 1, MOD_TN), lambda l, j: (l, 0, j)),
        ],
        out_specs=pl.BlockSpec((None, MOD_ROWS, MOD_TN), lambda l, j: (l, 0, j)),
        compiler_params=_cparams(("parallel", "parallel")),
        name="modulation",
    )(cvec, mod_w, mod_b.reshape(DEPTH, 1, n))


NORM_TM = 512


def _norm_kernel(h_ref, g_ref, shift_ref, scale_ref, o_ref):
    x = h_ref[...]
    ms = jnp.mean(x * x, axis=-1, keepdims=True)
    y = x * lax.rsqrt(ms + EPS) * g_ref[...]
    o_ref[...] = (y * (1.0 + scale_ref[...]) + shift_ref[...]).astype(BF)


def _norm_mod(h, g, mods, layer, k_shift, latent):
    rows = h.shape[0]
    ish = _mod_index(layer, k_shift, NORM_TM, latent)
    isc = _mod_index(layer, k_shift + 1, NORM_TM, latent)
    return pl.pallas_call(
        _norm_kernel,
        out_shape=jax.ShapeDtypeStruct((rows, D), BF),
        grid=(rows // NORM_TM,),
        in_specs=[
            pl.BlockSpec((NORM_TM, D), lambda i: (i, 0)),
            pl.BlockSpec((1, D), lambda i: (0, 0)),
            pl.BlockSpec((None, 1, D), lambda i: (ish(i), 0, 0)),
            pl.BlockSpec((None, 1, D), lambda i: (isc(i), 0, 0)),
        ],
        out_specs=pl.BlockSpec((NORM_TM, D), lambda i: (i, 0)),
        compiler_params=_cparams(("parallel",)),
        name="norm_mod",
    )(h, g, mods, mods)


MM_TM = 2048


def _resident_lhs_spec(tm, width):
    return pl.BlockSpec((tm, width), lambda i, j: (i, 0), pipeline_mode=pl.Buffered(1))


def _ffn_up_kernel(x_ref, w1_ref, w3_ref, *rest, w2_slabs):
    if w2_slabs:
        w2_ref, o_ref, w2p_ref = rest
    else:
        (o_ref,) = rest
    j = pl.program_id(1)

    @pl.when(j < N_FF_BLK)
    def _():
        x = x_ref[...]
        a = _dot(x, w1_ref[...].astype(BF))
        b = _dot(x, w3_ref[...].astype(BF))
        o_ref[...] = (a * jax.nn.sigmoid(a) * b).astype(BF)

    @pl.when(j == N_FF_BLK)
    def _():
        o_ref[...] = jnp.zeros_like(o_ref)

    if w2_slabs:
        s = pl.program_id(0) * pl.num_programs(1) + j

        @pl.when(s < w2_slabs)
        def _():
            w2p_ref[...] = w2_ref[...].astype(BF)

        @pl.when(s >= w2_slabs)
        def _():
            w2p_ref[...] = jnp.zeros_like(w2p_ref)


def _ffn_up(xn, w13, layer, w2=None):
    rows = xn.shape[0]
    last = N_FF_BLK - 1
    n_j = N_FF_BLK + 1
    in_specs = [
        _resident_lhs_spec(MM_TM, D),
        pl.BlockSpec((None, D, FF_BLK), lambda i, j: (layer, 0, jnp.minimum(j, last))),
        pl.BlockSpec((None, D, FF_BLK), lambda i, j: (layer, 0, N_FF_BLK + jnp.minimum(j, last))),
    ]
    out_shape = [jax.ShapeDtypeStruct((rows, FF_PAD), BF)]
    out_specs = [pl.BlockSpec((MM_TM, FF_BLK), lambda i, j: (i, j))]
    args = [xn, w13, w13]
    w2_slabs = 0
    if w2 is not None:
        n_steps = (rows // MM_TM) * n_j
        assert FF_PAD % n_steps == 0
        slab = FF_PAD // n_steps
        assert D_FF % slab == 0 and slab % 16 == 0
        w2_slabs = D_FF // slab
        in_specs.append(pl.BlockSpec((None, slab, D), lambda i, j: (layer, jnp.minimum(i * n_j + j, w2_slabs - 1), 0)))
        out_shape.append(jax.ShapeDtypeStruct((FF_PAD, D), BF))
        out_specs.append(pl.BlockSpec((slab, D), lambda i, j: (i * n_j + j, 0)))
        args.append(w2)
    outs = pl.pallas_call(
        functools.partial(_ffn_up_kernel, w2_slabs=w2_slabs),
        out_shape=out_shape,
        grid=(rows // MM_TM, n_j),
        in_specs=in_specs,
        out_specs=out_specs,
        compiler_params=_cparams(("arbitrary", "arbitrary")),
        name="ffn_up",
    )(*args)
    return outs if w2 is not None else outs[0]


DOWN_TM = 1024
DOWN_TN = 1024
DOWN_TK = FF_PAD // 4


SIDE_ROWS = 128


def _ffn_down_kernel(x_ref, w_ref, h_ref, gate_ref, *rest, side_every_k):
    if side_every_k is None:
        (o_ref,) = rest
    else:
        sh_ref, sg_ref, sshift_ref, sscale_ref, o_ref, sxn_ref = rest
    k = pl.program_id(2)
    last = pl.num_programs(2) - 1

    def side(first_k):
        if side_every_k or (first_k and side_every_k is not None):
            _norm_kernel(sh_ref, sg_ref, sshift_ref, sscale_ref, sxn_ref)

    @pl.when(k == 0)
    def _():
        o_ref[...] = _dot(x_ref[...], w_ref[...])
        side(True)

    @pl.when(jnp.logical_and(k > 0, k < last))
    def _():
        o_ref[...] += _dot(x_ref[...], w_ref[...])
        side(False)

    @pl.when(k == last)
    def _():
        acc = o_ref[...] + _dot(x_ref[...], w_ref[...])
        o_ref[...] = h_ref[...] + (0.5 * gate_ref[...]) * acc
        side(False)


def _ffn_down(x, w2p, h, mods, layer, k_gate, latent, side=None):
    rows = x.shape[0]
    n_j, n_k = D // DOWN_TN, FF_PAD // DOWN_TK
    ig = _mod_index(layer, k_gate, DOWN_TM, latent)
    in_specs = [
        pl.BlockSpec((DOWN_TM, DOWN_TK), lambda i, j, k: (i, k)),
        pl.BlockSpec((DOWN_TK, DOWN_TN), lambda i, j, k: (k, j)),
        pl.BlockSpec((DOWN_TM, DOWN_TN), lambda i, j, k: (i, j)),
        pl.BlockSpec((None, 1, DOWN_TN), lambda i, j, k: (ig(i), 0, j)),
    ]
    out_shape = [jax.ShapeDtypeStruct((rows, D), F32)]
    out_specs = [pl.BlockSpec((DOWN_TM, DOWN_TN), lambda i, j, k: (i, j))]
    args = [x, w2p, h, mods]
    side_every_k = None
    if side is not None:
        h_other, g, s_layer, k_shift = side
        n_chunks = h_other.shape[0] // SIDE_ROWS
        n_ij = (rows // DOWN_TM) * n_j
        side_every_k = n_chunks == n_ij * n_k
        assert side_every_k or n_chunks == n_ij
        if side_every_k:
            def chunk(i, j, k):
                return (i * n_j + j) * n_k + k
        else:
            def chunk(i, j, k):
                return i * n_j + j
        ish = _mod_index(s_layer, k_shift, SIDE_ROWS, not latent)
        isc = _mod_index(s_layer, k_shift + 1, SIDE_ROWS, not latent)
        in_specs += [
            pl.BlockSpec((SIDE_ROWS, D), lambda i, j, k: (chunk(i, j, k), 0)),
            pl.BlockSpec((1, D), lambda i, j, k: (0, 0)),
            pl.BlockSpec((None, 1, D), lambda i, j, k: (ish(chunk(i, j, k)), 0, 0)),
            pl.BlockSpec((None, 1, D), lambda i, j, k: (isc(chunk(i, j, k)), 0, 0)),
        ]
        out_shape.append(jax.ShapeDtypeStruct(h_other.shape, BF))
        out_specs.append(pl.BlockSpec((SIDE_ROWS, D), lambda i, j, k: (chunk(i, j, k), 0)))
        args += [h_other, g, mods, mods]
    outs = pl.pallas_call(
        functools.partial(_ffn_down_kernel, side_every_k=side_every_k),
        out_shape=out_shape,
        grid=(rows // DOWN_TM, n_j, n_k),
        in_specs=in_specs,
        out_specs=out_specs,
        compiler_params=_cparams(("arbitrary", "arbitrary", "arbitrary")),
        name="ffn_down",
    )(*args)
    return outs if side is not None else outs[0]


IN_TN = 512


def _in_proj_kernel(x_ref, w_ref, o_ref):
    o_ref[...] = _dot(x_ref[...], w_ref[...].astype(BF))


def _in_proj(u, w_in, layer):
    rows = u.shape[0]
    return pl.pallas_call(
        _in_proj_kernel,
        out_shape=jax.ShapeDtypeStruct((rows, IN_W), F32),
        grid=(rows // MM_TM, IN_W // IN_TN),
        in_specs=[
            _resident_lhs_spec(MM_TM, D),
            pl.BlockSpec((None, D, IN_TN), lambda i, j: (layer, 0, j)),
        ],
        out_specs=pl.BlockSpec((MM_TM, IN_TN), lambda i, j: (i, j)),
        compiler_params=_cparams(("parallel", "arbitrary")),
        name="in_proj",
    )(u, w_in)


OUT_TN = 512


def _out_proj_kernel(*refs, n_lhs):
    xs, ws = refs[:n_lhs], refs[n_lhs:2 * n_lhs]
    h_ref, gate_ref, o_ref = refs[2 * n_lhs:]
    acc = _dot(xs[0][...], ws[0][...].astype(BF))
    for x_ref, w_ref in zip(xs[1:], ws[1:]):
        acc += _dot(x_ref[...], w_ref[...].astype(BF))
    o_ref[...] = h_ref[...] + gate_ref[...] * acc


def _out_proj(xs, w_out, h, mods, layer, latent):
    rows = h.shape[0]
    ig = _mod_index(layer, 5, MM_TM, latent)
    x_specs, w_specs, c0 = [], [], 0
    for x in xs:
        w = x.shape[1]
        assert c0 % w == 0
        x_specs.append(_resident_lhs_spec(MM_TM, w))
        w_specs.append(pl.BlockSpec((None, w, OUT_TN), functools.partial(lambda i, j, r: (layer, r, j), r=c0 // w)))
        c0 += w
    assert c0 == MIX_W
    return pl.pallas_call(
        functools.partial(_out_proj_kernel, n_lhs=len(xs)),
        out_shape=jax.ShapeDtypeStruct((rows, D), F32),
        grid=(rows // MM_TM, D // OUT_TN),
        in_specs=x_specs + w_specs + [
            pl.BlockSpec((MM_TM, OUT_TN), lambda i, j: (i, j)),
            pl.BlockSpec((None, 1, OUT_TN), lambda i, j: (ig(i), 0, j)),
        ],
        out_specs=pl.BlockSpec((MM_TM, OUT_TN), lambda i, j: (i, j)),
        compiler_params=_cparams(("parallel", "arbitrary")),
        name="out_proj",
    )(*xs, *([w_out] * len(xs)), h, mods)


PREP_TM = 256
C_QA = 0
C_KA = C_QA + W_AQ
C_VA = C_KA + W_AKV
C_QB = C_VA + W_AKV
C_KB = C_QB + W_B
C_VB = C_KB + W_B


def _head_rms(x, g):
    ms = jnp.mean(x * x, axis=-1, keepdims=True)
    return x * lax.rsqrt(ms + EPS) * g


def _rope(x, cos, sin_signed, lo_lane):
    swapped = jnp.where(lo_lane, pltpu.roll(x, HD - 32, axis=1), pltpu.roll(x, 32, axis=1))
    return x * cos + swapped * sin_signed


def _prep_body(p_ref, nrm_ref, qa_ref, ka_ref, va_ref, qb_ref, kb_ref, vb_ref, rope, caches):
    g_aq = nrm_ref[0:1, :]
    g_ak = nrm_ref[1:2, :]
    g_bq = nrm_ref[2:3, :]
    g_bk = nrm_ref[3:4, :]
    if rope is not None:
        cos, sin_signed = rope
        lo_lane = (lax.broadcasted_iota(jnp.int32, (PREP_TM, HD), 1) % 64) < 32

    def col(c0, h):
        return p_ref[:, c0 + h * HD: c0 + (h + 1) * HD]

    for h in range(N_Q_A):
        y = _head_rms(col(C_QA, h), g_aq)
        if rope is not None:
            y = _rope(y, cos, sin_signed, lo_lane)
        qa_ref[:, h * HD:(h + 1) * HD] = y.astype(BF)
    for h in range(N_KV_A):
        y = _head_rms(col(C_KA, h), g_ak)
        if caches is not None:
            caches[0][:, h * HD:(h + 1) * HD] = y
        if rope is not None:
            y = _rope(y, cos, sin_signed, lo_lane)
        ka_ref[:, h * HD:(h + 1) * HD] = y.astype(BF)
    va = p_ref[:, C_VA:C_VA + W_AKV]
    va_ref[...] = va.astype(BF)
    if caches is not None:
        caches[1][...] = va
    for h in range(N_H_B):
        qb_ref[:, h * HD:(h + 1) * HD] = _head_rms(col(C_QB, h), g_bq).astype(BF)
        y = _head_rms(col(C_KB, h), g_bk)
        if caches is not None:
            caches[2][:, h * HD:(h + 1) * HD] = y
        kb_ref[:, h * HD:(h + 1) * HD] = y.astype(BF)
    vb = p_ref[:, C_VB:C_VB + W_B]
    vb_ref[...] = vb.astype(BF)
    if caches is not None:
        caches[3][...] = vb


def _prep_ctx_kernel(p_ref, nrm_ref, qa_ref, ka_ref, va_ref, qb_ref, kb_ref, vb_ref,
                     cak_ref, cav_ref, cbk_ref, cbv_ref):
    _prep_body(p_ref, nrm_ref, qa_ref, ka_ref, va_ref, qb_ref, kb_ref, vb_ref,
               None, (cak_ref, cav_ref, cbk_ref, cbv_ref))


def _prep_lat_kernel(p_ref, nrm_ref, cos_ref, sin_ref, qa_ref, ka_ref, va_ref, qb_ref, kb_ref, vb_ref):
    _prep_body(p_ref, nrm_ref, qa_ref, ka_ref, va_ref, qb_ref, kb_ref, vb_ref,
               (cos_ref[...], sin_ref[...]), None)


def _qkv_specs(rows):
    widths = (W_AQ, W_AKV, W_AKV, W_B, W_B, W_B)
    shapes = [jax.ShapeDtypeStruct((rows, w), BF) for w in widths]
    specs = [pl.BlockSpec((PREP_TM, w), lambda i: (i, 0)) for w in widths]
    return shapes, specs


def _prep_ctx(p, nrm):
    shapes, specs = _qkv_specs(TP)
    cache_w = (W_AKV, W_AKV, W_B, W_B)
    shapes += [jax.ShapeDtypeStruct((TP, w), F32) for w in cache_w]
    specs += [pl.BlockSpec((PREP_TM, w), lambda i: (i, 0)) for w in cache_w]
    return pl.pallas_call(
        _prep_ctx_kernel,
        out_shape=shapes,
        grid=(TP // PREP_TM,),
        in_specs=[
            pl.BlockSpec((PREP_TM, QKV_W), lambda i: (i, 0)),
            pl.BlockSpec((8, HD), lambda i: (0, 0)),
        ],
        out_specs=specs,
        compiler_params=_cparams(("parallel",)),
        name="prep_ctx",
    )(p, nrm)


def _prep_lat(p, nrm, cos, sin_signed):
    shapes, specs = _qkv_specs(TS)
    per_seq = DEC_SEQ // PREP_TM
    return pl.pallas_call(
        _prep_lat_kernel,
        out_shape=shapes,
        grid=(TS // PREP_TM,),
        in_specs=[
            pl.BlockSpec((PREP_TM, QKV_W), lambda i: (i, 0)),
            pl.BlockSpec((8, HD), lambda i: (0, 0)),
            pl.BlockSpec((PREP_TM, HD), lambda i: (i % per_seq, 0)),
            pl.BlockSpec((PREP_TM, HD), lambda i: (i % per_seq, 0)),
        ],
        out_specs=specs,
        compiler_params=_cparams(("parallel",)),
        name="prep_lat",
    )(p, nrm, cos, sin_signed)


def _conv_kernel(xc_ref, gb_ref, gc_ref, w_ref, b_ref, o_ref):
    u = gc_ref[...] * xc_ref[...]
    n = u.shape[0]
    row = lax.broadcasted_iota(jnp.int32, u.shape, 0)
    prev = jnp.where(row == 0, 0.0, pltpu.roll(u, 1, axis=0))
    nxt = jnp.where(row == n - 1, 0.0, pltpu.roll(u, n - 1, axis=0))
    y = w_ref[0:1, :] * prev + w_ref[1:2, :] * u + w_ref[2:3, :] * nxt + b_ref[...]
    o_ref[...] = (gb_ref[...] * y).astype(BF)


def _short_conv(p, conv_w, conv_b, rows, ch):
    n_seq = p.shape[0] // rows
    c0 = QKV_W // ch
    per = CONV_CH // ch
    return pl.pallas_call(
        _conv_kernel,
        out_shape=jax.ShapeDtypeStruct((n_seq * rows, CONV_CH), BF),
        grid=(n_seq, per),
        in_specs=[
            pl.BlockSpec((rows, ch), lambda s, c: (s, c0 + c)),
            pl.BlockSpec((rows, ch), lambda s, c: (s, c0 + per + c)),
            pl.BlockSpec((rows, ch), lambda s, c: (s, c0 + 2 * per + c)),
            pl.BlockSpec((8, ch), lambda s, c: (0, c)),
            pl.BlockSpec((1, ch), lambda s, c: (0, c)),
        ],
        out_specs=pl.BlockSpec((rows, ch), lambda s, c: (s, c)),
        compiler_params=_cparams(("parallel", "parallel")),
        name="short_conv",
    )(p, p, p, conv_w, conv_b)


def _softmax_parts(parts, sink):
    m = parts[0].max(axis=-1, keepdims=True)
    for s in parts[1:]:
        m = jnp.maximum(m, s.max(axis=-1, keepdims=True))
    if sink is not None:
        m = jnp.maximum(m, sink)
    es = [jnp.exp(s - m) for s in parts]
    den = es[0].sum(axis=-1, keepdims=True)
    for e in es[1:]:
        den = den + e.sum(axis=-1, keepdims=True)
    if sink is not None:
        den = den + jnp.exp(sink - m)
    inv = 1.0 / den
    return [(e * inv).astype(BF) for e in es]


def _sink_column(sink_ref, kv, rows):
    return jnp.concatenate(
        [jnp.full((rows, 1), sink_ref[kv * GQA + g], F32) for g in range(GQA)], axis=0)


def _group_rows(x, kv):
    return jnp.concatenate([x[:, (kv * GQA + g) * HD:(kv * GQA + g + 1) * HD] for g in range(GQA)], axis=0)


def _ungroup_rows(o, rows):
    return [o[g * rows:(g + 1) * rows] for g in range(GQA)]


def _attn_ctx_kernel(sink_ref, qa_ref, ka_ref, va_ref, qb_ref, kb_ref, vb_ref, o_ref):
    qa, ka, va = qa_ref[...], ka_ref[...], va_ref[...]
    qb, kb, vb = qb_ref[...], kb_ref[...], vb_ref[...]
    sls = [slice(h * HD, (h + 1) * HD) for h in range(N_H_B)]
    s_a = [_dot_nt(_group_rows(qa, kv), ka[:, sls[kv]]) * SCALE for kv in range(N_KV_A)]
    s_b = [_dot_nt(qb[:, sl], kb[:, sl]) * SCALE for sl in sls]
    p_a = [_softmax_parts([s], _sink_column(sink_ref, kv, SEQ))[0] for kv, s in enumerate(s_a)]
    p_b = [_softmax_parts([s], None)[0] for s in s_b]
    outs = []
    for kv, p in enumerate(p_a):
        outs += _ungroup_rows(_dot(p, va[:, sls[kv]]), SEQ)
    outs += [_dot(p, vb[:, sl]) for p, sl in zip(p_b, sls)]
    o_ref[...] = jnp.concatenate(outs, axis=1).astype(BF)


def _attn_ctx(sink, qa, ka, va, qb, kb, vb):
    def spec(w):
        return pl.BlockSpec((SEQ, w), lambda b: (b, 0))
    return pl.pallas_call(
        _attn_ctx_kernel,
        out_shape=jax.ShapeDtypeStruct((TP, W_AQ + W_B), BF),
        grid=(BATCH,),
        in_specs=[pl.BlockSpec(memory_space=pltpu.SMEM),
                  spec(W_AQ), spec(W_AKV), spec(W_AKV), spec(W_B), spec(W_B), spec(W_B)],
        out_specs=spec(W_AQ + W_B),
        compiler_params=_cparams(("parallel",)),
        name="attn_ctx",
    )(sink, qa, ka, va, qb, kb, vb)


QB = 128
SPAN = QB + 2 * WINDOW


def _attn_win_kernel(sink_ref, q_ref, k_ref, v_ref, ck_ref, cv_ref, o_ref):
    i = pl.program_id(1)
    start = i * QB
    s0 = pl.multiple_of(jnp.clip(start - WINDOW, 0, DEC_SEQ - SPAN), QB)
    qpos = start + lax.broadcasted_iota(jnp.int32, (GQA * QB, SPAN), 0) % QB
    kpos = s0 + lax.broadcasted_iota(jnp.int32, (GQA * QB, SPAN), 1)
    valid = jnp.abs(qpos - kpos) <= WINDOW
    q = q_ref[...]
    k = k_ref[pl.ds(s0, SPAN), :]
    v = v_ref[pl.ds(s0, SPAN), :]
    kk = jnp.concatenate([k, ck_ref[...]], axis=0)
    vv = jnp.concatenate([v, cv_ref[...]], axis=0)
    sls = [slice(kv * HD, (kv + 1) * HD) for kv in range(N_KV_A)]
    ss = [_dot_nt(_group_rows(q, kv), kk[:, sl]) * SCALE for kv, sl in enumerate(sls)]
    ss = [jnp.concatenate([jnp.where(valid, s[:, :SPAN], NEG_INF), s[:, SPAN:]], axis=1) for s in ss]
    ps = [_softmax_parts([s], _sink_column(sink_ref, kv, QB))[0] for kv, s in enumerate(ss)]
    outs = []
    for p, sl in zip(ps, sls):
        outs += _ungroup_rows(_dot(p, vv[:, sl]), QB)
    o_ref[...] = jnp.concatenate(outs, axis=1).astype(BF)


def _attn_win(sink, qa, ka, va, ck, cv):
    nb = DEC_SEQ // QB
    return pl.pallas_call(
        _attn_win_kernel,
        out_shape=jax.ShapeDtypeStruct((TS, W_AQ), BF),
        grid=(DEC_BATCH, nb),
        in_specs=[
            pl.BlockSpec(memory_space=pltpu.SMEM),
            pl.BlockSpec((QB, W_AQ), lambda b, i: (b * nb + i, 0)),
            pl.BlockSpec((DEC_SEQ, W_AKV), lambda b, i: (b, 0)),
            pl.BlockSpec((DEC_SEQ, W_AKV), lambda b, i: (b, 0)),
            pl.BlockSpec((None, PAST, W_AKV), lambda b, i: (b, 0, 0)),
            pl.BlockSpec((None, PAST, W_AKV), lambda b, i: (b, 0, 0)),
        ],
        out_specs=pl.BlockSpec((QB, W_AQ), lambda b, i: (b * nb + i, 0)),
        compiler_params=_cparams(("parallel", "arbitrary")),
        name="attn_win",
    )(sink, qa, ka, va, ck, cv)


NB_HG = 12
NB_KEYS = NA_ROWS * GRID_W


def _row_start(r):
    return jnp.clip(r - NA_ROWS // 2, 0, GRID_H - NA_ROWS)


def _attn_nbr_kernel(q_ref, k_ref, v_ref, ck_ref, cv_ref, bias_ref, o_ref):
    r = pl.program_id(2)
    k0 = pl.multiple_of(_row_start(r) * GRID_W, GRID_W)
    q = q_ref[...]
    k = k_ref[pl.ds(k0, NB_KEYS), :]
    v = v_ref[pl.ds(k0, NB_KEYS), :]
    kk = jnp.concatenate([k, ck_ref[...]], axis=0)
    vv = jnp.concatenate([v, cv_ref[...]], axis=0)
    sls = [slice(h * HD, (h + 1) * HD) for h in range(NB_HG)]
    ss = [_dot_nt(q[:, sl], kk[:, sl]) * SCALE for sl in sls]
    ss = [jnp.concatenate([s[:, :NB_KEYS] + bias_ref[h], s[:, NB_KEYS:]], axis=1) for h, s in enumerate(ss)]
    ps = [_softmax_parts([s], None)[0] for s in ss]
    outs = [_dot(p, vv[:, sl]) for p, sl in zip(ps, sls)]
    o_ref[...] = jnp.concatenate(outs, axis=1).astype(BF)


def _attn_nbr(qb, kb, vb, ck, cv, bias):
    hw = NB_HG * HD
    n_hg = N_H_B // NB_HG
    return pl.pallas_call(
        _attn_nbr_kernel,
        out_shape=jax.ShapeDtypeStruct((TS, W_B), BF),
        grid=(DEC_BATCH, n_hg, GRID_H),
        in_specs=[
            pl.BlockSpec((GRID_W, hw), lambda b, g, r: (b * GRID_H + r, g)),
            pl.BlockSpec((DEC_SEQ, hw), lambda b, g, r: (b, g), pipeline_mode=pl.Buffered(1)),
            pl.BlockSpec((DEC_SEQ, hw), lambda b, g, r: (b, g), pipeline_mode=pl.Buffered(1)),
            pl.BlockSpec((None, PAST, hw), lambda b, g, r: (b, 0, g)),
            pl.BlockSpec((None, PAST, hw), lambda b, g, r: (b, 0, g)),
            pl.BlockSpec((None, NB_HG, GRID_W, NB_KEYS), lambda b, g, r: (r - _row_start(r), g, 0, 0)),
        ],
        out_specs=pl.BlockSpec((GRID_W, hw), lambda b, g, r: (b * GRID_H + r, g)),
        compiler_params=_cparams(("parallel", "parallel", "arbitrary")),
        name="attn_nbr",
    )(qb, kb, vb, ck, cv, bias)


def _rope_tables():
    pairs = HD // 4
    t = np.arange(DEC_SEQ)
    pos = np.stack([t // GRID_W, t % GRID_W], axis=-1).astype(np.float32)
    inv = jnp.asarray(ROPE_BASE, F32) ** (-jnp.arange(pairs, dtype=F32) * 2.0 / (HD // 2))
    ang = jnp.asarray(pos)[:, :, None] * inv
    cos = jnp.cos(ang)
    sin = jnp.sin(ang)
    cos_l = jnp.concatenate([cos, cos], axis=-1).reshape(DEC_SEQ, HD)
    sin_l = jnp.concatenate([-sin, sin], axis=-1).reshape(DEC_SEQ, HD)
    return cos_l, sin_l


def _nbr_bias_table(rpb):
    edge = GRID_W - NA_COLS
    ext = jnp.concatenate([jnp.repeat(rpb[..., :1], edge, axis=-1), rpb.astype(F32),
                           jnp.repeat(rpb[..., -1:], edge, axis=-1)], axis=-1)
    toep = jnp.stack([ext[..., GRID_W - 1 - q: 2 * GRID_W - 1 - q] for q in range(GRID_W)], axis=-2)
    qc = np.arange(GRID_W)
    cs = np.clip(qc - NA_COLS // 2, 0, GRID_W - NA_COLS)
    col_ok = (qc[None, :] >= cs[:, None]) & (qc[None, :] < cs[:, None] + NA_COLS)
    toep = jnp.where(jnp.asarray(col_ok), toep, NEG_INF)
    tabs = []
    for s in range(NA_ROWS):
        blk = toep[:, NA_ROWS - 1 - s: 2 * NA_ROWS - 1 - s]
        tabs.append(jnp.transpose(blk, (0, 2, 1, 3)).reshape(N_H_B, GRID_W, NB_KEYS))
    return jnp.stack(tabs, axis=0)


def kernel(x_prompt, x_sample, cache_a_k, cache_a_v, cache_b_k, cache_b_v, c, c_ctx,
           mod_w, mod_b, norm_ffn1, norm_mix, norm_ffn2, ffn1_w13, ffn1_w2, ffn2_w13, ffn2_w2,
           w_in, w_out, a_q_norm, a_k_norm, a_sink, b_q_norm, b_k_norm, b_rpb, conv_w, conv_b):
    hp = x_prompt.reshape(TP, D)
    hs = x_sample.reshape(TS, D)

    cvec = jnp.concatenate([c_ctx[None, :], c, jnp.zeros((MOD_ROWS - N_GROUPS, D), F32)], axis=0)
    mods = _modulation(cvec, mod_w, mod_b)
    mods = mods[:, :N_GROUPS].reshape(DEPTH * N_GROUPS * N_MOD, 1, D)

    cos_l, sin_l = _rope_tables()

    def ffn_pair(hp, xn_p, hs, l, g, w13, w2, k0, nxt):
        g = g[l][None, :]
        hid, w2p = _ffn_up(xn_p, w13, l, w2)
        hp, xn_s = _ffn_down(hid, w2p, hp, mods, l, k0 + 2, False, side=(hs, g, l, k0))
        hid = _ffn_up(xn_s, w13, l)
        if nxt is None:
            return hp, None, _ffn_down(hid, w2p, hs, mods, l, k0 + 2, True)
        hs, xn_p = _ffn_down(hid, w2p, hs, mods, l, k0 + 2, True, side=(hp,) + nxt)
        return hp, xn_p, hs

    xn_p = _norm_mod(hp, norm_ffn1[0][None, :], mods, 0, 0, False)
    new_ak, new_av, new_bk, new_bv = [], [], [], []
    for l in range(DEPTH):
        g_mix = norm_mix[l][None, :]
        hp, xn_p, hs = ffn_pair(hp, xn_p, hs, l, norm_ffn1, ffn1_w13, ffn1_w2, 0, (g_mix, l, 3))

        nrm = jnp.concatenate([a_q_norm[l][None], a_k_norm[l][None], b_q_norm[l][None], b_k_norm[l][None],
                               jnp.zeros((4, HD), F32)], axis=0)
        cw = jnp.concatenate([conv_w[l], jnp.zeros((5, CONV_CH), F32)], axis=0)
        cb = conv_b[l][None, :]
        sink = a_sink[l]

        pp = _in_proj(xn_p, w_in, l)
        qa, ka, va, qb, kb, vb, cak, cav, cbk, cbv = _prep_ctx(pp, nrm)
        new_ak.append(cak.reshape(BATCH, SEQ, N_KV_A, HD))
        new_av.append(cav.reshape(BATCH, SEQ, N_KV_A, HD))
        new_bk.append(cbk.reshape(BATCH, SEQ, N_H_B, HD))
        new_bv.append(cbv.reshape(BATCH, SEQ, N_H_B, HD))
        o_ab = _attn_ctx(sink, qa, ka, va, qb, kb, vb)
        o_c = _short_conv(pp, cw, cb, SEQ, CONV_CH)
        hp = _out_proj([o_ab, o_c], w_out, hp, mods, l, False)

        ps = _in_proj(_norm_mod(hs, g_mix, mods, l, 3, True), w_in, l)
        qa, ka, va, qb, kb, vb = _prep_lat(ps, nrm, cos_l, sin_l)
        o_a = _attn_win(sink, qa, ka, va,
                        cache_a_k[:, l].reshape(DEC_BATCH, PAST, W_AKV).astype(BF),
                        cache_a_v[:, l].reshape(DEC_BATCH, PAST, W_AKV).astype(BF))
        o_b = _attn_nbr(qb, kb, vb,
                        cache_b_k[:, l].reshape(DEC_BATCH, PAST, W_B).astype(BF),
                        cache_b_v[:, l].reshape(DEC_BATCH, PAST, W_B).astype(BF),
                        _nbr_bias_table(b_rpb[l]))
        o_c = _short_conv(ps, cw, cb, DEC_SEQ, 128)
        hs = _out_proj([o_a, o_b, o_c], w_out, hs, mods, l, True)

        xn_p = _norm_mod(hp, norm_ffn2[l][None, :], mods, l, 6, False)
        nxt = (norm_ffn1[l + 1][None, :], l + 1, 0) if l + 1 < DEPTH else None
        hp, xn_p, hs = ffn_pair(hp, xn_p, hs, l, norm_ffn2, ffn2_w13, ffn2_w2, 6, nxt)

    return (hp.reshape(BATCH, SEQ, D), hs.reshape(DEC_BATCH, DEC_SEQ, D),
            jnp.stack(new_ak, axis=1), jnp.stack(new_av, axis=1),
            jnp.stack(new_bk, axis=1), jnp.stack(new_bv, axis=1))
```

```python
import functools

import jax
import jax.numpy as jnp
import numpy as np
from jax import lax
from jax.experimental import pallas as pl
from jax.experimental.pallas import tpu as pltpu

D = 4096
BATCH = 16
SEQ = 256
DEPTH = 2
DEC_BATCH = 2
DEC_SEQ = 4096
PAST = 512
GRID_W = 64
GRID_H = DEC_SEQ // GRID_W
HD = 128
N_Q_A = 12
N_KV_A = 4
GQA = N_Q_A // N_KV_A
WINDOW = 128
N_H_B = 12
NA_ROWS = 8
NA_COLS = 16
CONV_CH = 1024
D_FF = 11008
N_MOD = 9
ROPE_BASE = 10000.0
EPS = 1e-6
NEG_INF = -1e30
SCALE = HD ** -0.5

TP = BATCH * SEQ
TS = DEC_BATCH * DEC_SEQ
N_GROUPS = 1 + DEC_BATCH
W_AQ = N_Q_A * HD
W_AKV = N_KV_A * HD
W_B = N_H_B * HD
QKV_W = W_AQ + 2 * W_AKV + 3 * W_B
IN_W = QKV_W + 3 * CONV_CH
MIX_W = W_AQ + W_B + CONV_CH

FF_BLK = 256
N_FF_BLK = D_FF // FF_BLK
FF_PAD = (N_FF_BLK + 1) * FF_BLK

VMEM_LIMIT = 56 * 1024 * 1024

BF = jnp.bfloat16
F32 = jnp.float32


def _cparams(sem, vmem=VMEM_LIMIT):
    return pltpu.CompilerParams(dimension_semantics=sem, vmem_limit_bytes=vmem)


def _dot(a, b):
    return jnp.dot(a, b, preferred_element_type=F32)


def _dot_nt(a, b):
    return lax.dot_general(a, b, (((1,), (1,)), ((), ())), preferred_element_type=F32)


def _mod_index(layer, k, rows_per_block, latent):
    base = layer * N_GROUPS * N_MOD + k
    if not latent:
        return lambda i: base
    blocks_per_group = DEC_SEQ // rows_per_block
    return lambda i: base + (1 + i // blocks_per_group) * N_MOD


MOD_TN = 512
MOD_ROWS = 16


def _mod_kernel(c_ref, w_ref, b_ref, o_ref):
    c = c_ref[...]
    x = (c * jax.nn.sigmoid(c)).astype(BF)
    o_ref[...] = _dot(x, w_ref[...].astype(BF)) + b_ref[...]


def _modulation(cvec, mod_w, mod_b):
    n = N_MOD * D
    return pl.pallas_call(
        _mod_kernel,
        out_shape=jax.ShapeDtypeStruct((DEPTH, MOD_ROWS, n), F32),
        grid=(DEPTH, n // MOD_TN),
        in_specs=[
            pl.BlockSpec((MOD_ROWS, D), lambda l, j: (0, 0)),
            pl.BlockSpec((None, D, MOD_TN), lambda l, j: (l, 0, j)),
            pl.BlockSpec((None, 1, MOD_TN), lambda l, j: (l, 0, j)),
        ],
        out_specs=pl.BlockSpec((None, MOD_ROWS, MOD_TN), lambda l, j: (l, 0, j)),
        compiler_params=_cparams(("parallel", "parallel")),
        name="modulation",
    )(cvec, mod_w, mod_b.reshape(DEPTH, 1, n))


NORM_TM = 512


def _norm_kernel(h_ref, g_ref, shift_ref, scale_ref, o_ref):
    x = h_ref[...]
    ms = jnp.mean(x * x, axis=-1, keepdims=True)
    y = x * lax.rsqrt(ms + EPS) * g_ref[...]
    o_ref[...] = (y * (1.0 + scale_ref[...]) + shift_ref[...]).astype(BF)


def _norm_mod(h, g, mods, layer, k_shift, latent):
    rows = h.shape[0]
    ish = _mod_index(layer, k_shift, NORM_TM, latent)
    isc = _mod_index(layer, k_shift + 1, NORM_TM, latent)
    return pl.pallas_call(
        _norm_kernel,
        out_shape=jax.ShapeDtypeStruct((rows, D), BF),
        grid=(rows // NORM_TM,),
        in_specs=[
            pl.BlockSpec((NORM_TM, D), lambda i: (i, 0)),
            pl.BlockSpec((1, D), lambda i: (0, 0)),
            pl.BlockSpec((None, 1, D), lambda i: (ish(i), 0, 0)),
            pl.BlockSpec((None, 1, D), lambda i: (isc(i), 0, 0)),
        ],
        out_specs=pl.BlockSpec((NORM_TM, D), lambda i: (i, 0)),
        compiler_params=_cparams(("parallel",)),
        name="norm_mod",
    )(h, g, mods, mods)


MM_TM = 2048


def _resident_lhs_spec(tm, width):
    return pl.BlockSpec((tm, width), lambda i, j: (i, 0), pipeline_mode=pl.Buffered(1))


def _ffn_up_kernel(x_ref, w1_ref, w3_ref, *rest, w2_slabs):
    if w2_slabs:
        w2_ref, o_ref, w2p_ref = rest
    else:
        (o_ref,) = rest
    j = pl.program_id(1)

    @pl.when(j < N_FF_BLK)
    def _():
        x = x_ref[...]
        a = _dot(x, w1_ref[...].astype(BF))
        b = _dot(x, w3_ref[...].astype(BF))
        o_ref[...] = (a * jax.nn.sigmoid(a) * b).astype(BF)

    @pl.when(j == N_FF_BLK)
    def _():
        o_ref[...] = jnp.zeros_like(o_ref)

    if w2_slabs:
        s = pl.program_id(0) * pl.num_programs(1) + j

        @pl.when(s < w2_slabs)
        def _():
            w2p_ref[...] = w2_ref[...].astype(BF)

        @pl.when(s >= w2_slabs)
        def _():
            w2p_ref[...] = jnp.zeros_like(w2p_ref)


def _ffn_up(xn, w13, layer, w2=None):
    rows = xn.shape[0]
    last = N_FF_BLK - 1
    n_j = N_FF_BLK + 1
    in_specs = [
        _resident_lhs_spec(MM_TM, D),
        pl.BlockSpec((None, D, FF_BLK), lambda i, j: (layer, 0, jnp.minimum(j, last))),
        pl.BlockSpec((None, D, FF_BLK), lambda i, j: (layer, 0, N_FF_BLK + jnp.minimum(j, last))),
    ]
    out_shape = [jax.ShapeDtypeStruct((rows, FF_PAD), BF)]
    out_specs = [pl.BlockSpec((MM_TM, FF_BLK), lambda i, j: (i, j))]
    args = [xn, w13, w13]
    w2_slabs = 0
    if w2 is not None:
        n_steps = (rows // MM_TM) * n_j
        assert FF_PAD % n_steps == 0
        slab = FF_PAD // n_steps
        assert D_FF % slab == 0 and slab % 16 == 0
        w2_slabs = D_FF // slab
        in_specs.append(pl.BlockSpec((None, slab, D), lambda i, j: (layer, jnp.minimum(i * n_j + j, w2_slabs - 1), 0)))
        out_shape.append(jax.ShapeDtypeStruct((FF_PAD, D), BF))
        out_specs.append(pl.BlockSpec((slab, D), lambda i, j: (i * n_j + j, 0)))
        args.append(w2)
    outs = pl.pallas_call(
        functools.partial(_ffn_up_kernel, w2_slabs=w2_slabs),
        out_shape=out_shape,
        grid=(rows // MM_TM, n_j),
        in_specs=in_specs,
        out_specs=out_specs,
        compiler_params=_cparams(("arbitrary", "arbitrary")),
        name="ffn_up",
    )(*args)
    return outs if w2 is not None else outs[0]


DOWN_TM = 1024
DOWN_TN = 1024
DOWN_TK = FF_PAD // 4


SIDE_ROWS = 128


def _ffn_down_kernel(x_ref, w_ref, h_ref, gate_ref, *rest, side_every_k):
    if side_every_k is None:
        (o_ref,) = rest
    else:
        sh_ref, sg_ref, sshift_ref, sscale_ref, o_ref, sxn_ref = rest
    k = pl.program_id(2)
    last = pl.num_programs(2) - 1

    def side(first_k):
        if side_every_k or (first_k and side_every_k is not None):
            _norm_kernel(sh_ref, sg_ref, sshift_ref, sscale_ref, sxn_ref)

    @pl.when(k == 0)
    def _():
        o_ref[...] = _dot(x_ref[...], w_ref[...])
        side(True)

    @pl.when(jnp.logical_and(k > 0, k < last))
    def _():
        o_ref[...] += _dot(x_ref[...], w_ref[...])
        side(False)

    @pl.when(k == last)
    def _():
        acc = o_ref[...] + _dot(x_ref[...], w_ref[...])
        o_ref[...] = h_ref[...] + (0.5 * gate_ref[...]) * acc
        side(False)


def _ffn_down(x, w2p, h, mods, layer, k_gate, latent, side=None):
    rows = x.shape[0]
    n_j, n_k = D // DOWN_TN, FF_PAD // DOWN_TK
    ig = _mod_index(layer, k_gate, DOWN_TM, latent)
    in_specs = [
        pl.BlockSpec((DOWN_TM, DOWN_TK), lambda i, j, k: (i, k)),
        pl.BlockSpec((DOWN_TK, DOWN_TN), lambda i, j, k: (k, j)),
        pl.BlockSpec((DOWN_TM, DOWN_TN), lambda i, j, k: (i, j)),
        pl.BlockSpec((None, 1, DOWN_TN), lambda i, j, k: (ig(i), 0, j)),
    ]
    out_shape = [jax.ShapeDtypeStruct((rows, D), F32)]
    out_specs = [pl.BlockSpec((DOWN_TM, DOWN_TN), lambda i, j, k: (i, j))]
    args = [x, w2p, h, mods]
    side_every_k = None
    if side is not None:
        h_other, g, s_layer, k_shift = side
        n_chunks = h_other.shape[0] // SIDE_ROWS
        n_ij = (rows // DOWN_TM) * n_j
        side_every_k = n_chunks == n_ij * n_k
        assert side_every_k or n_chunks == n_ij
        if side_every_k:
            def chunk(i, j, k):
                return (i * n_j + j) * n_k + k
        else:
            def chunk(i, j, k):
                return i * n_j + j
        ish = _mod_index(s_layer, k_shift, SIDE_ROWS, not latent)
        isc = _mod_index(s_layer, k_shift + 1, SIDE_ROWS, not latent)
        in_specs += [
            pl.BlockSpec((SIDE_ROWS, D), lambda i, j, k: (chunk(i, j, k), 0)),
            pl.BlockSpec((1, D), lambda i, j, k: (0, 0)),
            pl.BlockSpec((None, 1, D), lambda i, j, k: (ish(chunk(i, j, k)), 0, 0)),
            pl.BlockSpec((None, 1, D), lambda i, j, k: (isc(chunk(i, j, k)), 0, 0)),
        ]
        out_shape.append(jax.ShapeDtypeStruct(h_other.shape, BF))
        out_specs.append(pl.BlockSpec((SIDE_ROWS, D), lambda i, j, k: (chunk(i, j, k), 0)))
        args += [h_other, g, mods, mods]
    outs = pl.pallas_call(
        functools.partial(_ffn_down_kernel, side_every_k=side_every_k),
        out_shape=out_shape,
        grid=(rows // DOWN_TM, n_j, n_k),
        in_specs=in_specs,
        out_specs=out_specs,
        compiler_params=_cparams(("arbitrary", "arbitrary", "arbitrary")),
        name="ffn_down",
    )(*args)
    return outs if side is not None else outs[0]


IN_TN = 512


def _in_proj_kernel(x_ref, w_ref, o_ref):
    o_ref[...] = _dot(x_ref[...], w_ref[...].astype(BF))


def _in_proj(u, w_in, layer):
    rows = u.shape[0]
    return pl.pallas_call(
        _in_proj_kernel,
        out_shape=jax.ShapeDtypeStruct((rows, IN_W), F32),
        grid=(rows // MM_TM, IN_W // IN_TN),
        in_specs=[
            _resident_lhs_spec(MM_TM, D),
            pl.BlockSpec((None, D, IN_TN), lambda i, j: (layer, 0, j)),
        ],
        out_specs=pl.BlockSpec((MM_TM, IN_TN), lambda i, j: (i, j)),
        compiler_params=_cparams(("parallel", "arbitrary")),
        name="in_proj",
    )(u, w_in)


OUT_TN = 512


def _out_proj_kernel(*refs, n_lhs):
    xs, ws = refs[:n_lhs], refs[n_lhs:2 * n_lhs]
    h_ref, gate_ref, o_ref = refs[2 * n_lhs:]
    acc = _dot(xs[0][...], ws[0][...].astype(BF))
    for x_ref, w_ref in zip(xs[1:], ws[1:]):
        acc += _dot(x_ref[...], w_ref[...].astype(BF))
    o_ref[...] = h_ref[...] + gate_ref[...] * acc


def _out_proj(xs, w_out, h, mods, layer, latent):
    rows = h.shape[0]
    ig = _mod_index(layer, 5, MM_TM, latent)
    x_specs, w_specs, c0 = [], [], 0
    for x in xs:
        w = x.shape[1]
        assert c0 % w == 0
        x_specs.append(_resident_lhs_spec(MM_TM, w))
        w_specs.append(pl.BlockSpec((None, w, OUT_TN), functools.partial(lambda i, j, r: (layer, r, j), r=c0 // w)))
        c0 += w
    assert c0 == MIX_W
    return pl.pallas_call(
        functools.partial(_out_proj_kernel, n_lhs=len(xs)),
        out_shape=jax.ShapeDtypeStruct((rows, D), F32),
        grid=(rows // MM_TM, D // OUT_TN),
        in_specs=x_specs + w_specs + [
            pl.BlockSpec((MM_TM, OUT_TN), lambda i, j: (i, j)),
            pl.BlockSpec((None, 1, OUT_TN), lambda i, j: (ig(i), 0, j)),
        ],
        out_specs=pl.BlockSpec((MM_TM, OUT_TN), lambda i, j: (i, j)),
        compiler_params=_cparams(("parallel", "arbitrary")),
        name="out_proj",
    )(*xs, *([w_out] * len(xs)), h, mods)


PREP_TM = 256
C_QA = 0
C_KA = C_QA + W_AQ
C_VA = C_KA + W_AKV
C_QB = C_VA + W_AKV
C_KB = C_QB + W_B
C_VB = C_KB + W_B


def _head_rms(x, g):
    ms = jnp.mean(x * x, axis=-1, keepdims=True)
    return x * lax.rsqrt(ms + EPS) * g


def _rope(x, cos, sin_signed, lo_lane):
    swapped = jnp.where(lo_lane, pltpu.roll(x, HD - 32, axis=1), pltpu.roll(x, 32, axis=1))
    return x * cos + swapped * sin_signed


def _prep_body(p_ref, nrm_ref, qa_ref, ka_ref, va_ref, qb_ref, kb_ref, vb_ref, rope, caches):
    g_aq = nrm_ref[0:1, :]
    g_ak = nrm_ref[1:2, :]
    g_bq = nrm_ref[2:3, :]
    g_bk = nrm_ref[3:4, :]
    if rope is not None:
        cos, sin_signed = rope
        lo_lane = (lax.broadcasted_iota(jnp.int32, (PREP_TM, HD), 1) % 64) < 32

    def col(c0, h):
        return p_ref[:, c0 + h * HD: c0 + (h + 1) * HD]

    for h in range(N_Q_A):
        y = _head_rms(col(C_QA, h), g_aq)
        if rope is not None:
            y = _rope(y, cos, sin_signed, lo_lane)
        qa_ref[:, h * HD:(h + 1) * HD] = y.astype(BF)
    for h in range(N_KV_A):
        y = _head_rms(col(C_KA, h), g_ak)
        if caches is not None:
            caches[0][:, h * HD:(h + 1) * HD] = y
        if rope is not None:
            y = _rope(y, cos, sin_signed, lo_lane)
        ka_ref[:, h * HD:(h + 1) * HD] = y.astype(BF)
    va = p_ref[:, C_VA:C_VA + W_AKV]
    va_ref[...] = va.astype(BF)
    if caches is not None:
        caches[1][...] = va
    for h in range(N_H_B):
        qb_ref[:, h * HD:(h + 1) * HD] = _head_rms(col(C_QB, h), g_bq).astype(BF)
        y = _head_rms(col(C_KB, h), g_bk)
        if caches is not None:
            caches[2][:, h * HD:(h + 1) * HD] = y
        kb_ref[:, h * HD:(h + 1) * HD] = y.astype(BF)
    vb = p_ref[:, C_VB:C_VB + W_B]
    vb_ref[...] = vb.astype(BF)
    if caches is not None:
        caches[3][...] = vb


def _prep_ctx_kernel(p_ref, nrm_ref, qa_ref, ka_ref, va_ref, qb_ref, kb_ref, vb_ref,
                     cak_ref, cav_ref, cbk_ref, cbv_ref):
    _prep_body(p_ref, nrm_ref, qa_ref, ka_ref, va_ref, qb_ref, kb_ref, vb_ref,
               None, (cak_ref, cav_ref, cbk_ref, cbv_ref))


def _prep_lat_kernel(p_ref, nrm_ref, cos_ref, sin_ref, qa_ref, ka_ref, va_ref, qb_ref, kb_ref, vb_ref):
    _prep_body(p_ref, nrm_ref, qa_ref, ka_ref, va_ref, qb_ref, kb_ref, vb_ref,
               (cos_ref[...], sin_ref[...]), None)


def _qkv_specs(rows):
    widths = (W_AQ, W_AKV, W_AKV, W_B, W_B, W_B)
    shapes = [jax.ShapeDtypeStruct((rows, w), BF) for w in widths]
    specs = [pl.BlockSpec((PREP_TM, w), lambda i: (i, 0)) for w in widths]
    return shapes, specs


def _prep_ctx(p, nrm):
    shapes, specs = _qkv_specs(TP)
    cache_w = (W_AKV, W_AKV, W_B, W_B)
    shapes += [jax.ShapeDtypeStruct((TP, w), F32) for w in cache_w]
    specs += [pl.BlockSpec((PREP_TM, w), lambda i: (i, 0)) for w in cache_w]
    return pl.pallas_call(
        _prep_ctx_kernel,
        out_shape=shapes,
        grid=(TP // PREP_TM,),
        in_specs=[
            pl.BlockSpec((PREP_TM, QKV_W), lambda i: (i, 0)),
            pl.BlockSpec((8, HD), lambda i: (0, 0)),
        ],
        out_specs=specs,
        compiler_params=_cparams(("parallel",)),
        name="prep_ctx",
    )(p, nrm)


def _prep_lat(p, nrm, cos, sin_signed):
    shapes, specs = _qkv_specs(TS)
    per_seq = DEC_SEQ // PREP_TM
    return pl.pallas_call(
        _prep_lat_kernel,
        out_shape=shapes,
        grid=(TS // PREP_TM,),
        in_specs=[
            pl.BlockSpec((PREP_TM, QKV_W), lambda i: (i, 0)),
            pl.BlockSpec((8, HD), lambda i: (0, 0)),
            pl.BlockSpec((PREP_TM, HD), lambda i: (i % per_seq, 0)),
            pl.BlockSpec((PREP_TM, HD), lambda i: (i % per_seq, 0)),
        ],
        out_specs=specs,
        compiler_params=_cparams(("parallel",)),
        name="prep_lat",
    )(p, nrm, cos, sin_signed)


def _conv_kernel(xc_ref, gb_ref, gc_ref, w_ref, b_ref, o_ref):
    u = gc_ref[...] * xc_ref[...]
    n = u.shape[0]
    row = lax.broadcasted_iota(jnp.int32, u.shape, 0)
    prev = jnp.where(row == 0, 0.0, pltpu.roll(u, 1, axis=0))
    nxt = jnp.where(row == n - 1, 0.0, pltpu.roll(u, n - 1, axis=0))
    y = w_ref[0:1, :] * prev + w_ref[1:2, :] * u + w_ref[2:3, :] * nxt + b_ref[...]
    o_ref[...] = (gb_ref[...] * y).astype(BF)


def _short_conv(p, conv_w, conv_b, rows, ch):
    n_seq = p.shape[0] // rows
    c0 = QKV_W // ch
    per = CONV_CH // ch
    return pl.pallas_call(
        _conv_kernel,
        out_shape=jax.ShapeDtypeStruct((n_seq * rows, CONV_CH), BF),
        grid=(n_seq, per),
        in_specs=[
            pl.BlockSpec((rows, ch), lambda s, c: (s, c0 + c)),
            pl.BlockSpec((rows, ch), lambda s, c: (s, c0 + per + c)),
            pl.BlockSpec((rows, ch), lambda s, c: (s, c0 + 2 * per + c)),
            pl.BlockSpec((8, ch), lambda s, c: (0, c)),
            pl.BlockSpec((1, ch), lambda s, c: (0, c)),
        ],
        out_specs=pl.BlockSpec((rows, ch), lambda s, c: (s, c)),
        compiler_params=_cparams(("parallel", "parallel")),
        name="short_conv",
    )(p, p, p, conv_w, conv_b)


def _softmax_parts(parts, sink):
    m = parts[0].max(axis=-1, keepdims=True)
    for s in parts[1:]:
        m = jnp.maximum(m, s.max(axis=-1, keepdims=True))
    if sink is not None:
        m = jnp.maximum(m, sink)
    es = [jnp.exp(s - m) for s in parts]
    den = es[0].sum(axis=-1, keepdims=True)
    for e in es[1:]:
        den = den + e.sum(axis=-1, keepdims=True)
    if sink is not None:
        den = den + jnp.exp(sink - m)
    inv = 1.0 / den
    return [(e * inv).astype(BF) for e in es]


def _sink_column(sink_ref, kv, rows):
    return jnp.concatenate(
        [jnp.full((rows, 1), sink_ref[kv * GQA + g], F32) for g in range(GQA)], axis=0)


def _group_rows(x, kv):
    return jnp.concatenate([x[:, (kv * GQA + g) * HD:(kv * GQA + g + 1) * HD] for g in range(GQA)], axis=0)


def _ungroup_rows(o, rows):
    return [o[g * rows:(g + 1) * rows] for g in range(GQA)]


def _attn_ctx_kernel(sink_ref, qa_ref, ka_ref, va_ref, qb_ref, kb_ref, vb_ref, o_ref):
    qa, ka, va = qa_ref[...], ka_ref[...], va_ref[...]
    qb, kb, vb = qb_ref[...], kb_ref[...], vb_ref[...]
    sls = [slice(h * HD, (h + 1) * HD) for h in range(N_H_B)]
    s_a = [_dot_nt(_group_rows(qa, kv), ka[:, sls[kv]]) * SCALE for kv in range(N_KV_A)]
    s_b = [_dot_nt(qb[:, sl], kb[:, sl]) * SCALE for sl in sls]
    p_a = [_softmax_parts([s], _sink_column(sink_ref, kv, SEQ))[0] for kv, s in enumerate(s_a)]
    p_b = [_softmax_parts([s], None)[0] for s in s_b]
    outs = []
    for kv, p in enumerate(p_a):
        outs += _ungroup_rows(_dot(p, va[:, sls[kv]]), SEQ)
    outs += [_dot(p, vb[:, sl]) for p, sl in zip(p_b, sls)]
    o_ref[...] = jnp.concatenate(outs, axis=1).astype(BF)


def _attn_ctx(sink, qa, ka, va, qb, kb, vb):
    def spec(w):
        return pl.BlockSpec((SEQ, w), lambda b: (b, 0))
    return pl.pallas_call(
        _attn_ctx_kernel,
        out_shape=jax.ShapeDtypeStruct((TP, W_AQ + W_B), BF),
        grid=(BATCH,),
        in_specs=[pl.BlockSpec(memory_space=pltpu.SMEM),
                  spec(W_AQ), spec(W_AKV), spec(W_AKV), spec(W_B), spec(W_B), spec(W_B)],
        out_specs=spec(W_AQ + W_B),
        compiler_params=_cparams(("parallel",)),
        name="attn_ctx",
    )(sink, qa, ka, va, qb, kb, vb)


QB = 128
SPAN = QB + 2 * WINDOW


WIN_BLOCKS = 2


def _attn_win_kernel(sink_ref, q_ref, k_ref, v_ref, ck_ref, cv_ref, o_ref):
    t = pl.program_id(1)
    ck, cv = ck_ref[...], cv_ref[...]
    sls = [slice(kv * HD, (kv + 1) * HD) for kv in range(N_KV_A)]
    qs, kks, vvs, valids = [], [], [], []
    for u in range(WIN_BLOCKS):
        start = (t * WIN_BLOCKS + u) * QB
        s0 = pl.multiple_of(jnp.clip(start - WINDOW, 0, DEC_SEQ - SPAN), QB)
        qpos = start + lax.broadcasted_iota(jnp.int32, (GQA * QB, SPAN), 0) % QB
        kpos = s0 + lax.broadcasted_iota(jnp.int32, (GQA * QB, SPAN), 1)
        valids.append(jnp.abs(qpos - kpos) <= WINDOW)
        qs.append(q_ref[u * QB:(u + 1) * QB, :])
        kks.append(jnp.concatenate([k_ref[pl.ds(s0, SPAN), :], ck], axis=0))
        vvs.append(jnp.concatenate([v_ref[pl.ds(s0, SPAN), :], cv], axis=0))
    ss = [[_dot_nt(_group_rows(q, kv), kk[:, sl]) * SCALE for kv, sl in enumerate(sls)] for q, kk in zip(qs, kks)]
    ss = [[jnp.concatenate([jnp.where(valid, s[:, :SPAN], NEG_INF), s[:, SPAN:]], axis=1) for s in row]
          for row, valid in zip(ss, valids)]
    ps = [[_softmax_parts([s], _sink_column(sink_ref, kv, QB))[0] for kv, s in enumerate(row)] for row in ss]
    blocks = []
    for row, vv in zip(ps, vvs):
        outs = []
        for p, sl in zip(row, sls):
            outs += _ungroup_rows(_dot(p, vv[:, sl]), QB)
        blocks.append(jnp.concatenate(outs, axis=1))
    o_ref[...] = jnp.concatenate(blocks, axis=0).astype(BF)


def _attn_win(sink, qa, ka, va, ck, cv):
    nb = DEC_SEQ // (QB * WIN_BLOCKS)
    return pl.pallas_call(
        _attn_win_kernel,
        out_shape=jax.ShapeDtypeStruct((TS, W_AQ), BF),
        grid=(DEC_BATCH, nb),
        in_specs=[
            pl.BlockSpec(memory_space=pltpu.SMEM),
            pl.BlockSpec((QB * WIN_BLOCKS, W_AQ), lambda b, i: (b * nb + i, 0)),
            pl.BlockSpec((DEC_SEQ, W_AKV), lambda b, i: (b, 0)),
            pl.BlockSpec((DEC_SEQ, W_AKV), lambda b, i: (b, 0)),
            pl.BlockSpec((None, PAST, W_AKV), lambda b, i: (b, 0, 0)),
            pl.BlockSpec((None, PAST, W_AKV), lambda b, i: (b, 0, 0)),
        ],
        out_specs=pl.BlockSpec((QB * WIN_BLOCKS, W_AQ), lambda b, i: (b * nb + i, 0)),
        compiler_params=_cparams(("parallel", "arbitrary")),
        name="attn_win",
    )(sink, qa, ka, va, ck, cv)


NB_HG = 12
NB_KEYS = NA_ROWS * GRID_W


def _row_start(r):
    return jnp.clip(r - NA_ROWS // 2, 0, GRID_H - NA_ROWS)


NB_ROWS = 2


def _attn_nbr_kernel(q_ref, k_ref, v_ref, ck_ref, cv_ref, *rest):
    bias_refs, o_ref = rest[:NB_ROWS], rest[NB_ROWS]
    t = pl.program_id(2)
    ck, cv = ck_ref[...], cv_ref[...]
    sls = [slice(h * HD, (h + 1) * HD) for h in range(NB_HG)]
    qs, kks, vvs = [], [], []
    for u in range(NB_ROWS):
        k0 = pl.multiple_of(_row_start(t * NB_ROWS + u) * GRID_W, GRID_W)
        qs.append(q_ref[u * GRID_W:(u + 1) * GRID_W, :])
        kks.append(jnp.concatenate([k_ref[pl.ds(k0, NB_KEYS), :], ck], axis=0))
        vvs.append(jnp.concatenate([v_ref[pl.ds(k0, NB_KEYS), :], cv], axis=0))
    ss = [[_dot_nt(q[:, sl], kk[:, sl]) * SCALE for sl in sls] for q, kk in zip(qs, kks)]
    ss = [[jnp.concatenate([s[:, :NB_KEYS] + bias_refs[u][h], s[:, NB_KEYS:]], axis=1) for h, s in enumerate(row)]
          for u, row in enumerate(ss)]
    ps = [[_softmax_parts([s], None)[0] for s in row] for row in ss]
    outs = [jnp.concatenate([_dot(p, vv[:, sl]) for p, sl in zip(row, sls)], axis=1) for row, vv in zip(ps, vvs)]
    o_ref[...] = jnp.concatenate(outs, axis=0).astype(BF)


def _attn_nbr(qb, kb, vb, ck, cv, bias):
    hw = NB_HG * HD
    n_hg = N_H_B // NB_HG
    steps = GRID_H // NB_ROWS
    rows = NB_ROWS * GRID_W

    def bias_spec(u):
        def index(b, g, t):
            r = t * NB_ROWS + u
            return (r - _row_start(r), g, 0, 0)
        return pl.BlockSpec((None, NB_HG, GRID_W, NB_KEYS), index)

    return pl.pallas_call(
        _attn_nbr_kernel,
        out_shape=jax.ShapeDtypeStruct((TS, W_B), BF),
        grid=(DEC_BATCH, n_hg, steps),
        in_specs=[
            pl.BlockSpec((rows, hw), lambda b, g, t: (b * steps + t, g)),
            pl.BlockSpec((DEC_SEQ, hw), lambda b, g, t: (b, g), pipeline_mode=pl.Buffered(1)),
            pl.BlockSpec((DEC_SEQ, hw), lambda b, g, t: (b, g), pipeline_mode=pl.Buffered(1)),
            pl.BlockSpec((None, PAST, hw), lambda b, g, t: (b, 0, g)),
            pl.BlockSpec((None, PAST, hw), lambda b, g, t: (b, 0, g)),
        ] + [bias_spec(u) for u in range(NB_ROWS)],
        out_specs=pl.BlockSpec((rows, hw), lambda b, g, t: (b * steps + t, g)),
        compiler_params=_cparams(("parallel", "parallel", "arbitrary")),
        name="attn_nbr",
    )(qb, kb, vb, ck, cv, *([bias] * NB_ROWS))


def _rope_tables():
    pairs = HD // 4
    t = np.arange(DEC_SEQ)
    pos = np.stack([t // GRID_W, t % GRID_W], axis=-1).astype(np.float32)
    inv = jnp.asarray(ROPE_BASE, F32) ** (-jnp.arange(pairs, dtype=F32) * 2.0 / (HD // 2))
    ang = jnp.asarray(pos)[:, :, None] * inv
    cos = jnp.cos(ang)
    sin = jnp.sin(ang)
    cos_l = jnp.concatenate([cos, cos], axis=-1).reshape(DEC_SEQ, HD)
    sin_l = jnp.concatenate([-sin, sin], axis=-1).reshape(DEC_SEQ, HD)
    return cos_l, sin_l


def _nbr_bias_table(rpb):
    edge = GRID_W - NA_COLS
    ext = jnp.concatenate([jnp.repeat(rpb[..., :1], edge, axis=-1), rpb.astype(F32),
                           jnp.repeat(rpb[..., -1:], edge, axis=-1)], axis=-1)
    toep = jnp.stack([ext[..., GRID_W - 1 - q: 2 * GRID_W - 1 - q] for q in range(GRID_W)], axis=-2)
    qc = np.arange(GRID_W)
    cs = np.clip(qc - NA_COLS // 2, 0, GRID_W - NA_COLS)
    col_ok = (qc[None, :] >= cs[:, None]) & (qc[None, :] < cs[:, None] + NA_COLS)
    toep = jnp.where(jnp.asarray(col_ok), toep, NEG_INF)
    tabs = []
    for s in range(NA_ROWS):
        blk = toep[:, NA_ROWS - 1 - s: 2 * NA_ROWS - 1 - s]
        tabs.append(jnp.transpose(blk, (0, 2, 1, 3)).reshape(N_H_B, GRID_W, NB_KEYS))
    return jnp.stack(tabs, axis=0)


def kernel(x_prompt, x_sample, cache_a_k, cache_a_v, cache_b_k, cache_b_v, c, c_ctx,
           mod_w, mod_b, norm_ffn1, norm_mix, norm_ffn2, ffn1_w13, ffn1_w2, ffn2_w13, ffn2_w2,
           w_in, w_out, a_q_norm, a_k_norm, a_sink, b_q_norm, b_k_norm, b_rpb, conv_w, conv_b):
    hp = x_prompt.reshape(TP, D)
    hs = x_sample.reshape(TS, D)

    cvec = jnp.concatenate([c_ctx[None, :], c, jnp.zeros((MOD_ROWS - N_GROUPS, D), F32)], axis=0)
    mods = _modulation(cvec, mod_w, mod_b)
    mods = mods[:, :N_GROUPS].reshape(DEPTH * N_GROUPS * N_MOD, 1, D)

    cos_l, sin_l = _rope_tables()

    def ffn_pair(hp, xn_p, hs, l, g, w13, w2, k0, nxt):
        g = g[l][None, :]
        hid, w2p = _ffn_up(xn_p, w13, l, w2)
        hp, xn_s = _ffn_down(hid, w2p, hp, mods, l, k0 + 2, False, side=(hs, g, l, k0))
        hid = _ffn_up(xn_s, w13, l)
        if nxt is None:
            return hp, None, _ffn_down(hid, w2p, hs, mods, l, k0 + 2, True)
        hs, xn_p = _ffn_down(hid, w2p, hs, mods, l, k0 + 2, True, side=(hp,) + nxt)
        return hp, xn_p, hs

    xn_p = _norm_mod(hp, norm_ffn1[0][None, :], mods, 0, 0, False)
    new_ak, new_av, new_bk, new_bv = [], [], [], []
    for l in range(DEPTH):
        g_mix = norm_mix[l][None, :]
        hp, xn_p, hs = ffn_pair(hp, xn_p, hs, l, norm_ffn1, ffn1_w13, ffn1_w2, 0, (g_mix, l, 3))

        nrm = jnp.concatenate([a_q_norm[l][None], a_k_norm[l][None], b_q_norm[l][None], b_k_norm[l][None],
                               jnp.zeros((4, HD), F32)], axis=0)
        cw = jnp.concatenate([conv_w[l], jnp.zeros((5, CONV_CH), F32)], axis=0)
        cb = conv_b[l][None, :]
        sink = a_sink[l]

        pp = _in_proj(xn_p, w_in, l)
        qa, ka, va, qb, kb, vb, cak, cav, cbk, cbv = _prep_ctx(pp, nrm)
        new_ak.append(cak.reshape(BATCH, SEQ, N_KV_A, HD))
        new_av.append(cav.reshape(BATCH, SEQ, N_KV_A, HD))
        new_bk.append(cbk.reshape(BATCH, SEQ, N_H_B, HD))
        new_bv.append(cbv.reshape(BATCH, SEQ, N_H_B, HD))
        o_ab = _attn_ctx(sink, qa, ka, va, qb, kb, vb)
        o_c = _short_conv(pp, cw, cb, SEQ, CONV_CH)
        hp = _out_proj([o_ab, o_c], w_out, hp, mods, l, False)

        ps = _in_proj(_norm_mod(hs, g_mix, mods, l, 3, True), w_in, l)
        qa, ka, va, qb, kb, vb = _prep_lat(ps, nrm, cos_l, sin_l)
        o_a = _attn_win(sink, qa, ka, va,
                        cache_a_k[:, l].reshape(DEC_BATCH, PAST, W_AKV).astype(BF),
                        cache_a_v[:, l].reshape(DEC_BATCH, PAST, W_AKV).astype(BF))
        o_b = _attn_nbr(qb, kb, vb,
                        cache_b_k[:, l].reshape(DEC_BATCH, PAST, W_B).astype(BF),
                        cache_b_v[:, l].reshape(DEC_BATCH, PAST, W_B).astype(BF),
                        _nbr_bias_table(b_rpb[l]))
        o_c = _short_conv(ps, cw, cb, DEC_SEQ, 128)
        hs = _out_proj([o_a, o_b, o_c], w_out, hs, mods, l, True)

        xn_p = _norm_mod(hp, norm_ffn2[l][None, :], mods, l, 6, False)
        nxt = (norm_ffn1[l + 1][None, :], l + 1, 0) if l + 1 < DEPTH else None
        hp, xn_p, hs = ffn_pair(hp, xn_p, hs, l, norm_ffn2, ffn2_w13, ffn2_w2, 6, nxt)

    return (hp.reshape(BATCH, SEQ, D), hs.reshape(DEC_BATCH, DEC_SEQ, D),
            jnp.stack(new_ak, axis=1), jnp.stack(new_av, axis=1),
            jnp.stack(new_bk, axis=1), jnp.stack(new_bv, axis=1))
```

```python
import functools

import jax
import jax.numpy as jnp
import numpy as np
from jax import lax
from jax.experimental import pallas as pl
from jax.experimental.pallas import tpu as pltpu

D = 4096
BATCH = 16
SEQ = 256
DEPTH = 2
DEC_BATCH = 2
DEC_SEQ = 4096
PAST = 512
GRID_W = 64
GRID_H = DEC_SEQ // GRID_W
HD = 128
N_Q_A = 12
N_KV_A = 4
GQA = N_Q_A // N_KV_A
WINDOW = 128
N_H_B = 12
NA_ROWS = 8
NA_COLS = 16
CONV_CH = 1024
D_FF = 11008
N_MOD = 9
ROPE_BASE = 10000.0
EPS = 1e-6
NEG_INF = -1e30
SCALE = HD ** -0.5

TP = BATCH * SEQ
TS = DEC_BATCH * DEC_SEQ
N_GROUPS = 1 + DEC_BATCH
W_AQ = N_Q_A * HD
W_AKV = N_KV_A * HD
W_B = N_H_B * HD
QKV_W = W_AQ + 2 * W_AKV + 3 * W_B
IN_W = QKV_W + 3 * CONV_CH
MIX_W = W_AQ + W_B + CONV_CH

FF_BLK = 256
N_FF_BLK = D_FF // FF_BLK
FF_PAD = (N_FF_BLK + 1) * FF_BLK

VMEM_LIMIT = 56 * 1024 * 1024

BF = jnp.bfloat16
F32 = jnp.float32


def _cparams(sem, vmem=VMEM_LIMIT):
    return pltpu.CompilerParams(dimension_semantics=sem, vmem_limit_bytes=vmem)


def _dot(a, b):
    return jnp.dot(a, b, preferred_element_type=F32)


def _dot_nt(a, b):
    return lax.dot_general(a, b, (((1,), (1,)), ((), ())), preferred_element_type=F32)


def _mod_index(layer, k, rows_per_block, latent):
    base = layer * N_GROUPS * N_MOD + k
    if not latent:
        return lambda i: base
    blocks_per_group = DEC_SEQ // rows_per_block
    return lambda i: base + (1 + i // blocks_per_group) * N_MOD


MOD_TN = 512
MOD_ROWS = 16


def _mod_kernel(c_ref, w_ref, b_ref, o_ref):
    c = c_ref[...]
    x = (c * jax.nn.sigmoid(c)).astype(BF)
    o_ref[...] = _dot(x, w_ref[...].astype(BF)) + b_ref[...]


def _modulation(cvec, mod_w, mod_b):
    n = N_MOD * D
    return pl.pallas_call(
        _mod_kernel,
        out_shape=jax.ShapeDtypeStruct((DEPTH, MOD_ROWS, n), F32),
        grid=(DEPTH, n // MOD_TN),
        in_specs=[
            pl.BlockSpec((MOD_ROWS, D), lambda l, j: (0, 0)),
            pl.BlockSpec((None, D, MOD_TN), lambda l, j: (l, 0, j)),
            pl.BlockSpec((None, 1, MOD_TN), lambda l, j: (l, 0, j)),
        ],
        out_specs=pl.BlockSpec((None, MOD_ROWS, MOD_TN), lambda l, j: (l, 0, j)),
        compiler_params=_cparams(("parallel", "parallel")),
        name="modulation",
    )(cvec, mod_w, mod_b.reshape(DEPTH, 1, n))


NORM_TM = 512


def _norm_kernel(h_ref, g_ref, shift_ref, scale_ref, o_ref):
    x = h_ref[...]
    ms = jnp.mean(x * x, axis=-1, keepdims=True)
    y = x * lax.rsqrt(ms + EPS) * g_ref[...]
    o_ref[...] = (y * (1.0 + scale_ref[...]) + shift_ref[...]).astype(BF)


def _norm_mod(h, g, mods, layer, k_shift, latent):
    rows = h.shape[0]
    ish = _mod_index(layer, k_shift, NORM_TM, latent)
    isc = _mod_index(layer, k_shift + 1, NORM_TM, latent)
    return pl.pallas_call(
        _norm_kernel,
        out_shape=jax.ShapeDtypeStruct((rows, D), BF),
        grid=(rows // NORM_TM,),
        in_specs=[
            pl.BlockSpec((NORM_TM, D), lambda i: (i, 0)),
            pl.BlockSpec((1, D), lambda i: (0, 0)),
            pl.BlockSpec((None, 1, D), lambda i: (ish(i), 0, 0)),
            pl.BlockSpec((None, 1, D), lambda i: (isc(i), 0, 0)),
        ],
        out_specs=pl.BlockSpec((NORM_TM, D), lambda i: (i, 0)),
        compiler_params=_cparams(("parallel",)),
        name="norm_mod",
    )(h, g, mods, mods)


MM_TM = 2048


def _resident_lhs_spec(tm, width):
    return pl.BlockSpec((tm, width), lambda i, j: (i, 0), pipeline_mode=pl.Buffered(1))


def _ffn_up_kernel(x_ref, w1_ref, w3_ref, *rest, w2_slabs):
    if w2_slabs:
        w2_ref, o_ref, w2p_ref = rest
    else:
        (o_ref,) = rest
    j = pl.program_id(1)

    @pl.when(j < N_FF_BLK)
    def _():
        x = x_ref[...]
        a = _dot(x, w1_ref[...].astype(BF))
        b = _dot(x, w3_ref[...].astype(BF))
        o_ref[...] = (a * jax.nn.sigmoid(a) * b).astype(BF)

    @pl.when(j == N_FF_BLK)
    def _():
        o_ref[...] = jnp.zeros_like(o_ref)

    if w2_slabs:
        s = pl.program_id(0) * pl.num_programs(1) + j

        @pl.when(s < w2_slabs)
        def _():
            w2p_ref[...] = w2_ref[...].astype(BF)

        @pl.when(s >= w2_slabs)
        def _():
            w2p_ref[...] = jnp.zeros_like(w2p_ref)


def _ffn_up(xn, w13, layer, w2=None):
    rows = xn.shape[0]
    last = N_FF_BLK - 1
    n_j = N_FF_BLK + 1
    in_specs = [
        pl.BlockSpec((MM_TM, D), lambda i, j: (i, 0)),
        pl.BlockSpec((None, D, FF_BLK), lambda i, j: (layer, 0, jnp.minimum(j, last))),
        pl.BlockSpec((None, D, FF_BLK), lambda i, j: (layer, 0, N_FF_BLK + jnp.minimum(j, last))),
    ]
    out_shape = [jax.ShapeDtypeStruct((rows, FF_PAD), BF)]
    out_specs = [pl.BlockSpec((MM_TM, FF_BLK), lambda i, j: (i, j))]
    args = [xn, w13, w13]
    w2_slabs = 0
    if w2 is not None:
        n_steps = (rows // MM_TM) * n_j
        assert FF_PAD % n_steps == 0
        slab = FF_PAD // n_steps
        assert D_FF % slab == 0 and slab % 16 == 0
        w2_slabs = D_FF // slab
        in_specs.append(pl.BlockSpec((None, slab, D), lambda i, j: (layer, jnp.minimum(i * n_j + j, w2_slabs - 1), 0)))
        out_shape.append(jax.ShapeDtypeStruct((FF_PAD, D), BF))
        out_specs.append(pl.BlockSpec((slab, D), lambda i, j: (i * n_j + j, 0)))
        args.append(w2)
    outs = pl.pallas_call(
        functools.partial(_ffn_up_kernel, w2_slabs=w2_slabs),
        out_shape=out_shape,
        grid=(rows // MM_TM, n_j),
        in_specs=in_specs,
        out_specs=out_specs,
        compiler_params=_cparams(("arbitrary", "arbitrary"), 62 * 1024 * 1024),
        name="ffn_up",
    )(*args)
    return outs if w2 is not None else outs[0]


DOWN_TM = 1024
DOWN_TN = 1024
DOWN_TK = FF_PAD // 4


SIDE_ROWS = 128


def _ffn_down_kernel(x_ref, w_ref, h_ref, gate_ref, *rest, side_every_k):
    if side_every_k is None:
        (o_ref,) = rest
    else:
        sh_ref, sg_ref, sshift_ref, sscale_ref, o_ref, sxn_ref = rest
    k = pl.program_id(2)
    last = pl.num_programs(2) - 1

    def side(first_k):
        if side_every_k or (first_k and side_every_k is not None):
            _norm_kernel(sh_ref, sg_ref, sshift_ref, sscale_ref, sxn_ref)

    @pl.when(k == 0)
    def _():
        o_ref[...] = _dot(x_ref[...], w_ref[...])
        side(True)

    @pl.when(jnp.logical_and(k > 0, k < last))
    def _():
        o_ref[...] += _dot(x_ref[...], w_ref[...])
        side(False)

    @pl.when(k == last)
    def _():
        acc = o_ref[...] + _dot(x_ref[...], w_ref[...])
        o_ref[...] = h_ref[...] + (0.5 * gate_ref[...]) * acc
        side(False)


def _ffn_down(x, w2p, h, mods, layer, k_gate, latent, side=None):
    rows = x.shape[0]
    n_j, n_k = D // DOWN_TN, FF_PAD // DOWN_TK
    ig = _mod_index(layer, k_gate, DOWN_TM, latent)
    in_specs = [
        pl.BlockSpec((DOWN_TM, DOWN_TK), lambda i, j, k: (i, k)),
        pl.BlockSpec((DOWN_TK, DOWN_TN), lambda i, j, k: (k, j)),
        pl.BlockSpec((DOWN_TM, DOWN_TN), lambda i, j, k: (i, j)),
        pl.BlockSpec((None, 1, DOWN_TN), lambda i, j, k: (ig(i), 0, j)),
    ]
    out_shape = [jax.ShapeDtypeStruct((rows, D), F32)]
    out_specs = [pl.BlockSpec((DOWN_TM, DOWN_TN), lambda i, j, k: (i, j))]
    args = [x, w2p, h, mods]
    side_every_k = None
    if side is not None:
        h_other, g, s_layer, k_shift = side
        n_chunks = h_other.shape[0] // SIDE_ROWS
        n_ij = (rows // DOWN_TM) * n_j
        side_every_k = n_chunks == n_ij * n_k
        assert side_every_k or n_chunks == n_ij
        if side_every_k:
            def chunk(i, j, k):
                return (i * n_j + j) * n_k + k
        else:
            def chunk(i, j, k):
                return i * n_j + j
        ish = _mod_index(s_layer, k_shift, SIDE_ROWS, not latent)
        isc = _mod_index(s_layer, k_shift + 1, SIDE_ROWS, not latent)
        in_specs += [
            pl.BlockSpec((SIDE_ROWS, D), lambda i, j, k: (chunk(i, j, k), 0)),
            pl.BlockSpec((1, D), lambda i, j, k: (0, 0)),
            pl.BlockSpec((None, 1, D), lambda i, j, k: (ish(chunk(i, j, k)), 0, 0)),
            pl.BlockSpec((None, 1, D), lambda i, j, k: (isc(chunk(i, j, k)), 0, 0)),
        ]
        out_shape.append(jax.ShapeDtypeStruct(h_other.shape, BF))
        out_specs.append(pl.BlockSpec((SIDE_ROWS, D), lambda i, j, k: (chunk(i, j, k), 0)))
        args += [h_other, g, mods, mods]
    outs = pl.pallas_call(
        functools.partial(_ffn_down_kernel, side_every_k=side_every_k),
        out_shape=out_shape,
        grid=(rows // DOWN_TM, n_j, n_k),
        in_specs=in_specs,
        out_specs=out_specs,
        compiler_params=_cparams(("arbitrary", "arbitrary", "arbitrary")),
        name="ffn_down",
    )(*args)
    return outs if side is not None else outs[0]


IN_TN = 512


def _in_proj_kernel(x_ref, w_ref, o_ref):
    o_ref[...] = _dot(x_ref[...], w_ref[...].astype(BF))


def _in_proj(u, w_in, layer):
    rows = u.shape[0]
    return pl.pallas_call(
        _in_proj_kernel,
        out_shape=jax.ShapeDtypeStruct((rows, IN_W), F32),
        grid=(rows // MM_TM, IN_W // IN_TN),
        in_specs=[
            pl.BlockSpec((MM_TM, D), lambda i, j: (i, 0)),
            pl.BlockSpec((None, D, IN_TN), lambda i, j: (layer, 0, j)),
        ],
        out_specs=pl.BlockSpec((MM_TM, IN_TN), lambda i, j: (i, j)),
        compiler_params=_cparams(("parallel", "arbitrary"), 62 * 1024 * 1024),
        name="in_proj",
    )(u, w_in)


OUT_TN = 512


def _out_proj_kernel(*refs, n_lhs):
    xs, ws = refs[:n_lhs], refs[n_lhs:2 * n_lhs]
    h_ref, gate_ref, o_ref = refs[2 * n_lhs:]
    acc = _dot(xs[0][...], ws[0][...].astype(BF))
    for x_ref, w_ref in zip(xs[1:], ws[1:]):
        acc += _dot(x_ref[...], w_ref[...].astype(BF))
    o_ref[...] = h_ref[...] + gate_ref[...] * acc


def _out_proj(xs, w_out, h, mods, layer, latent):
    rows = h.shape[0]
    ig = _mod_index(layer, 5, MM_TM, latent)
    x_specs, w_specs, c0 = [], [], 0
    for x in xs:
        w = x.shape[1]
        assert c0 % w == 0
        x_specs.append(_resident_lhs_spec(MM_TM, w))
        w_specs.append(pl.BlockSpec((None, w, OUT_TN), functools.partial(lambda i, j, r: (layer, r, j), r=c0 // w)))
        c0 += w
    assert c0 == MIX_W
    return pl.pallas_call(
        functools.partial(_out_proj_kernel, n_lhs=len(xs)),
        out_shape=jax.ShapeDtypeStruct((rows, D), F32),
        grid=(rows // MM_TM, D // OUT_TN),
        in_specs=x_specs + w_specs + [
            pl.BlockSpec((MM_TM, OUT_TN), lambda i, j: (i, j)),
            pl.BlockSpec((None, 1, OUT_TN), lambda i, j: (ig(i), 0, j)),
        ],
        out_specs=pl.BlockSpec((MM_TM, OUT_TN), lambda i, j: (i, j)),
        compiler_params=_cparams(("parallel", "arbitrary")),
        name="out_proj",
    )(*xs, *([w_out] * len(xs)), h, mods)


PREP_TM = 256
C_QA = 0
C_KA = C_QA + W_AQ
C_VA = C_KA + W_AKV
C_QB = C_VA + W_AKV
C_KB = C_QB + W_B
C_VB = C_KB + W_B


def _head_rms(x, g):
    ms = jnp.mean(x * x, axis=-1, keepdims=True)
    return x * lax.rsqrt(ms + EPS) * g


def _rope(x, cos, sin_signed, lo_lane):
    swapped = jnp.where(lo_lane, pltpu.roll(x, HD - 32, axis=1), pltpu.roll(x, 32, axis=1))
    return x * cos + swapped * sin_signed


def _prep_body(p_ref, nrm_ref, qa_ref, ka_ref, va_ref, qb_ref, kb_ref, vb_ref, rope, caches):
    g_aq = nrm_ref[0:1, :]
    g_ak = nrm_ref[1:2, :]
    g_bq = nrm_ref[2:3, :]
    g_bk = nrm_ref[3:4, :]
    if rope is not None:
        cos, sin_signed = rope
        lo_lane = (lax.broadcasted_iota(jnp.int32, (PREP_TM, HD), 1) % 64) < 32

    def col(c0, h):
        return p_ref[:, c0 + h * HD: c0 + (h + 1) * HD]

    for h in range(N_Q_A):
        y = _head_rms(col(C_QA, h), g_aq)
        if rope is not None:
            y = _rope(y, cos, sin_signed, lo_lane)
        qa_ref[:, h * HD:(h + 1) * HD] = y.astype(BF)
    for h in range(N_KV_A):
        y = _head_rms(col(C_KA, h), g_ak)
        if caches is not None:
            caches[0][:, h * HD:(h + 1) * HD] = y
        if rope is not None:
            y = _rope(y, cos, sin_signed, lo_lane)
        ka_ref[:, h * HD:(h + 1) * HD] = y.astype(BF)
    va = p_ref[:, C_VA:C_VA + W_AKV]
    va_ref[...] = va.astype(BF)
    if caches is not None:
        caches[1][...] = va
    for h in range(N_H_B):
        qb_ref[:, h * HD:(h + 1) * HD] = _head_rms(col(C_QB, h), g_bq).astype(BF)
        y = _head_rms(col(C_KB, h), g_bk)
        if caches is not None:
            caches[2][:, h * HD:(h + 1) * HD] = y
        kb_ref[:, h * HD:(h + 1) * HD] = y.astype(BF)
    vb = p_ref[:, C_VB:C_VB + W_B]
    vb_ref[...] = vb.astype(BF)
    if caches is not None:
        caches[3][...] = vb


def _prep_ctx_kernel(p_ref, nrm_ref, qa_ref, ka_ref, va_ref, qb_ref, kb_ref, vb_ref,
                     cak_ref, cav_ref, cbk_ref, cbv_ref):
    _prep_body(p_ref, nrm_ref, qa_ref, ka_ref, va_ref, qb_ref, kb_ref, vb_ref,
               None, (cak_ref, cav_ref, cbk_ref, cbv_ref))


def _prep_lat_kernel(p_ref, nrm_ref, cos_ref, sin_ref, qa_ref, ka_ref, va_ref, qb_ref, kb_ref, vb_ref):
    _prep_body(p_ref, nrm_ref, qa_ref, ka_ref, va_ref, qb_ref, kb_ref, vb_ref,
               (cos_ref[...], sin_ref[...]), None)


def _qkv_specs(rows):
    widths = (W_AQ, W_AKV, W_AKV, W_B, W_B, W_B)
    shapes = [jax.ShapeDtypeStruct((rows, w), BF) for w in widths]
    specs = [pl.BlockSpec((PREP_TM, w), lambda i: (i, 0)) for w in widths]
    return shapes, specs


def _prep_ctx(p, nrm):
    shapes, specs = _qkv_specs(TP)
    cache_w = (W_AKV, W_AKV, W_B, W_B)
    shapes += [jax.ShapeDtypeStruct((TP, w), F32) for w in cache_w]
    specs += [pl.BlockSpec((PREP_TM, w), lambda i: (i, 0)) for w in cache_w]
    return pl.pallas_call(
        _prep_ctx_kernel,
        out_shape=shapes,
        grid=(TP // PREP_TM,),
        in_specs=[
            pl.BlockSpec((PREP_TM, QKV_W), lambda i: (i, 0)),
            pl.BlockSpec((8, HD), lambda i: (0, 0)),
        ],
        out_specs=specs,
        compiler_params=_cparams(("parallel",)),
        name="prep_ctx",
    )(p, nrm)


def _prep_lat(p, nrm, cos, sin_signed):
    shapes, specs = _qkv_specs(TS)
    per_seq = DEC_SEQ // PREP_TM
    return pl.pallas_call(
        _prep_lat_kernel,
        out_shape=shapes,
        grid=(TS // PREP_TM,),
        in_specs=[
            pl.BlockSpec((PREP_TM, QKV_W), lambda i: (i, 0)),
            pl.BlockSpec((8, HD), lambda i: (0, 0)),
            pl.BlockSpec((PREP_TM, HD), lambda i: (i % per_seq, 0)),
            pl.BlockSpec((PREP_TM, HD), lambda i: (i % per_seq, 0)),
        ],
        out_specs=specs,
        compiler_params=_cparams(("parallel",)),
        name="prep_lat",
    )(p, nrm, cos, sin_signed)


def _conv_kernel(xc_ref, gb_ref, gc_ref, w_ref, b_ref, o_ref):
    u = gc_ref[...] * xc_ref[...]
    n = u.shape[0]
    row = lax.broadcasted_iota(jnp.int32, u.shape, 0)
    prev = jnp.where(row == 0, 0.0, pltpu.roll(u, 1, axis=0))
    nxt = jnp.where(row == n - 1, 0.0, pltpu.roll(u, n - 1, axis=0))
    y = w_ref[0:1, :] * prev + w_ref[1:2, :] * u + w_ref[2:3, :] * nxt + b_ref[...]
    o_ref[...] = (gb_ref[...] * y).astype(BF)


def _short_conv(p, conv_w, conv_b, rows, ch):
    n_seq = p.shape[0] // rows
    c0 = QKV_W // ch
    per = CONV_CH // ch
    return pl.pallas_call(
        _conv_kernel,
        out_shape=jax.ShapeDtypeStruct((n_seq * rows, CONV_CH), BF),
        grid=(n_seq, per),
        in_specs=[
            pl.BlockSpec((rows, ch), lambda s, c: (s, c0 + c)),
            pl.BlockSpec((rows, ch), lambda s, c: (s, c0 + per + c)),
            pl.BlockSpec((rows, ch), lambda s, c: (s, c0 + 2 * per + c)),
            pl.BlockSpec((8, ch), lambda s, c: (0, c)),
            pl.BlockSpec((1, ch), lambda s, c: (0, c)),
        ],
        out_specs=pl.BlockSpec((rows, ch), lambda s, c: (s, c)),
        compiler_params=_cparams(("parallel", "parallel")),
        name="short_conv",
    )(p, p, p, conv_w, conv_b)


def _softmax_parts(parts, sink):
    m = parts[0].max(axis=-1, keepdims=True)
    for s in parts[1:]:
        m = jnp.maximum(m, s.max(axis=-1, keepdims=True))
    if sink is not None:
        m = jnp.maximum(m, sink)
    es = [jnp.exp(s - m) for s in parts]
    den = es[0].sum(axis=-1, keepdims=True)
    for e in es[1:]:
        den = den + e.sum(axis=-1, keepdims=True)
    if sink is not None:
        den = den + jnp.exp(sink - m)
    inv = 1.0 / den
    return [(e * inv).astype(BF) for e in es]


def _sink_column(sink_ref, kv, rows):
    return jnp.concatenate(
        [jnp.full((rows, 1), sink_ref[kv * GQA + g], F32) for g in range(GQA)], axis=0)


def _group_rows(x, kv):
    return jnp.concatenate([x[:, (kv * GQA + g) * HD:(kv * GQA + g + 1) * HD] for g in range(GQA)], axis=0)


def _ungroup_rows(o, rows):
    return [o[g * rows:(g + 1) * rows] for g in range(GQA)]


def _attn_ctx_kernel(sink_ref, qa_ref, ka_ref, va_ref, qb_ref, kb_ref, vb_ref, o_ref):
    qa, ka, va = qa_ref[...], ka_ref[...], va_ref[...]
    qb, kb, vb = qb_ref[...], kb_ref[...], vb_ref[...]
    sls = [slice(h * HD, (h + 1) * HD) for h in range(N_H_B)]
    s_a = [_dot_nt(_group_rows(qa, kv), ka[:, sls[kv]]) * SCALE for kv in range(N_KV_A)]
    s_b = [_dot_nt(qb[:, sl], kb[:, sl]) * SCALE for sl in sls]
    p_a = [_softmax_parts([s], _sink_column(sink_ref, kv, SEQ))[0] for kv, s in enumerate(s_a)]
    p_b = [_softmax_parts([s], None)[0] for s in s_b]
    outs = []
    for kv, p in enumerate(p_a):
        outs += _ungroup_rows(_dot(p, va[:, sls[kv]]), SEQ)
    outs += [_dot(p, vb[:, sl]) for p, sl in zip(p_b, sls)]
    o_ref[...] = jnp.concatenate(outs, axis=1).astype(BF)


def _attn_ctx(sink, qa, ka, va, qb, kb, vb):
    def spec(w):
        return pl.BlockSpec((SEQ, w), lambda b: (b, 0))
    return pl.pallas_call(
        _attn_ctx_kernel,
        out_shape=jax.ShapeDtypeStruct((TP, W_AQ + W_B), BF),
        grid=(BATCH,),
        in_specs=[pl.BlockSpec(memory_space=pltpu.SMEM),
                  spec(W_AQ), spec(W_AKV), spec(W_AKV), spec(W_B), spec(W_B), spec(W_B)],
        out_specs=spec(W_AQ + W_B),
        compiler_params=_cparams(("parallel",)),
        name="attn_ctx",
    )(sink, qa, ka, va, qb, kb, vb)


QB = 128
SPAN = QB + 2 * WINDOW


WIN_BLOCKS = 2


def _attn_win_kernel(sink_ref, q_ref, k_ref, v_ref, ck_ref, cv_ref, o_ref):
    t = pl.program_id(1)
    ck, cv = ck_ref[...], cv_ref[...]
    sls = [slice(kv * HD, (kv + 1) * HD) for kv in range(N_KV_A)]
    qs, kks, vvs, valids = [], [], [], []
    for u in range(WIN_BLOCKS):
        start = (t * WIN_BLOCKS + u) * QB
        s0 = pl.multiple_of(jnp.clip(start - WINDOW, 0, DEC_SEQ - SPAN), QB)
        qpos = start + lax.broadcasted_iota(jnp.int32, (GQA * QB, SPAN), 0) % QB
        kpos = s0 + lax.broadcasted_iota(jnp.int32, (GQA * QB, SPAN), 1)
        valids.append(jnp.abs(qpos - kpos) <= WINDOW)
        qs.append(q_ref[u * QB:(u + 1) * QB, :])
        kks.append(jnp.concatenate([k_ref[pl.ds(s0, SPAN), :], ck], axis=0))
        vvs.append(jnp.concatenate([v_ref[pl.ds(s0, SPAN), :], cv], axis=0))
    ss = [[_dot_nt(_group_rows(q, kv), kk[:, sl]) * SCALE for kv, sl in enumerate(sls)] for q, kk in zip(qs, kks)]
    ss = [[jnp.concatenate([jnp.where(valid, s[:, :SPAN], NEG_INF), s[:, SPAN:]], axis=1) for s in row]
          for row, valid in zip(ss, valids)]
    ps = [[_softmax_parts([s], _sink_column(sink_ref, kv, QB))[0] for kv, s in enumerate(row)] for row in ss]
    blocks = []
    for row, vv in zip(ps, vvs):
        outs = []
        for p, sl in zip(row, sls):
            outs += _ungroup_rows(_dot(p, vv[:, sl]), QB)
        blocks.append(jnp.concatenate(outs, axis=1))
    o_ref[...] = jnp.concatenate(blocks, axis=0).astype(BF)


def _attn_win(sink, qa, ka, va, ck, cv):
    nb = DEC_SEQ // (QB * WIN_BLOCKS)
    return pl.pallas_call(
        _attn_win_kernel,
        out_shape=jax.ShapeDtypeStruct((TS, W_AQ), BF),
        grid=(DEC_BATCH, nb),
        in_specs=[
            pl.BlockSpec(memory_space=pltpu.SMEM),
            pl.BlockSpec((QB * WIN_BLOCKS, W_AQ), lambda b, i: (b * nb + i, 0)),
            pl.BlockSpec((DEC_SEQ, W_AKV), lambda b, i: (b, 0)),
            pl.BlockSpec((DEC_SEQ, W_AKV), lambda b, i: (b, 0)),
            pl.BlockSpec((None, PAST, W_AKV), lambda b, i: (b, 0, 0)),
            pl.BlockSpec((None, PAST, W_AKV), lambda b, i: (b, 0, 0)),
        ],
        out_specs=pl.BlockSpec((QB * WIN_BLOCKS, W_AQ), lambda b, i: (b * nb + i, 0)),
        compiler_params=_cparams(("parallel", "arbitrary")),
        name="attn_win",
    )(sink, qa, ka, va, ck, cv)


NB_HG = 12
NB_KEYS = NA_ROWS * GRID_W


def _row_start(r):
    return jnp.clip(r - NA_ROWS // 2, 0, GRID_H - NA_ROWS)


NB_ROWS = 2


def _attn_nbr_kernel(q_ref, k_ref, v_ref, ck_ref, cv_ref, *rest):
    bias_refs, o_ref = rest[:NB_ROWS], rest[NB_ROWS]
    t = pl.program_id(2)
    ck, cv = ck_ref[...], cv_ref[...]
    sls = [slice(h * HD, (h + 1) * HD) for h in range(NB_HG)]
    qs, kks, vvs = [], [], []
    for u in range(NB_ROWS):
        k0 = pl.multiple_of(_row_start(t * NB_ROWS + u) * GRID_W, GRID_W)
        qs.append(q_ref[u * GRID_W:(u + 1) * GRID_W, :])
        kks.append(jnp.concatenate([k_ref[pl.ds(k0, NB_KEYS), :], ck], axis=0))
        vvs.append(jnp.concatenate([v_ref[pl.ds(k0, NB_KEYS), :], cv], axis=0))
    ss = [[_dot_nt(q[:, sl], kk[:, sl]) * SCALE for sl in sls] for q, kk in zip(qs, kks)]
    ss = [[jnp.concatenate([s[:, :NB_KEYS] + bias_refs[u][h], s[:, NB_KEYS:]], axis=1) for h, s in enumerate(row)]
          for u, row in enumerate(ss)]
    ps = [[_softmax_parts([s], None)[0] for s in row] for row in ss]
    outs = [jnp.concatenate([_dot(p, vv[:, sl]) for p, sl in zip(row, sls)], axis=1) for row, vv in zip(ps, vvs)]
    o_ref[...] = jnp.concatenate(outs, axis=0).astype(BF)


def _attn_nbr(qb, kb, vb, ck, cv, bias):
    hw = NB_HG * HD
    n_hg = N_H_B // NB_HG
    steps = GRID_H // NB_ROWS
    rows = NB_ROWS * GRID_W

    def bias_spec(u):
        def index(b, g, t):
            r = t * NB_ROWS + u
            return (r - _row_start(r), g, 0, 0)
        return pl.BlockSpec((None, NB_HG, GRID_W, NB_KEYS), index)

    return pl.pallas_call(
        _attn_nbr_kernel,
        out_shape=jax.ShapeDtypeStruct((TS, W_B), BF),
        grid=(DEC_BATCH, n_hg, steps),
        in_specs=[
            pl.BlockSpec((rows, hw), lambda b, g, t: (b * steps + t, g)),
            pl.BlockSpec((DEC_SEQ, hw), lambda b, g, t: (b, g), pipeline_mode=pl.Buffered(1)),
            pl.BlockSpec((DEC_SEQ, hw), lambda b, g, t: (b, g), pipeline_mode=pl.Buffered(1)),
            pl.BlockSpec((None, PAST, hw), lambda b, g, t: (b, 0, g)),
            pl.BlockSpec((None, PAST, hw), lambda b, g, t: (b, 0, g)),
        ] + [bias_spec(u) for u in range(NB_ROWS)],
        out_specs=pl.BlockSpec((rows, hw), lambda b, g, t: (b * steps + t, g)),
        compiler_params=_cparams(("parallel", "parallel", "arbitrary")),
        name="attn_nbr",
    )(qb, kb, vb, ck, cv, *([bias] * NB_ROWS))


def _rope_tables():
    pairs = HD // 4
    t = np.arange(DEC_SEQ)
    pos = np.stack([t // GRID_W, t % GRID_W], axis=-1).astype(np.float32)
    inv = jnp.asarray(ROPE_BASE, F32) ** (-jnp.arange(pairs, dtype=F32) * 2.0 / (HD // 2))
    ang = jnp.asarray(pos)[:, :, None] * inv
    cos = jnp.cos(ang)
    sin = jnp.sin(ang)
    cos_l = jnp.concatenate([cos, cos], axis=-1).reshape(DEC_SEQ, HD)
    sin_l = jnp.concatenate([-sin, sin], axis=-1).reshape(DEC_SEQ, HD)
    return cos_l, sin_l


def _nbr_bias_table(rpb):
    edge = GRID_W - NA_COLS
    ext = jnp.concatenate([jnp.repeat(rpb[..., :1], edge, axis=-1), rpb.astype(F32),
                           jnp.repeat(rpb[..., -1:], edge, axis=-1)], axis=-1)
    toep = jnp.stack([ext[..., GRID_W - 1 - q: 2 * GRID_W - 1 - q] for q in range(GRID_W)], axis=-2)
    qc = np.arange(GRID_W)
    cs = np.clip(qc - NA_COLS // 2, 0, GRID_W - NA_COLS)
    col_ok = (qc[None, :] >= cs[:, None]) & (qc[None, :] < cs[:, None] + NA_COLS)
    toep = jnp.where(jnp.asarray(col_ok), toep, NEG_INF)
    tabs = []
    for s in range(NA_ROWS):
        blk = toep[:, NA_ROWS - 1 - s: 2 * NA_ROWS - 1 - s]
        tabs.append(jnp.transpose(blk, (0, 2, 1, 3)).reshape(N_H_B, GRID_W, NB_KEYS))
    return jnp.stack(tabs, axis=0)


def kernel(x_prompt, x_sample, cache_a_k, cache_a_v, cache_b_k, cache_b_v, c, c_ctx,
           mod_w, mod_b, norm_ffn1, norm_mix, norm_ffn2, ffn1_w13, ffn1_w2, ffn2_w13, ffn2_w2,
           w_in, w_out, a_q_norm, a_k_norm, a_sink, b_q_norm, b_k_norm, b_rpb, conv_w, conv_b):
    hp = x_prompt.reshape(TP, D)
    hs = x_sample.reshape(TS, D)

    cvec = jnp.concatenate([c_ctx[None, :], c, jnp.zeros((MOD_ROWS - N_GROUPS, D), F32)], axis=0)
    mods = _modulation(cvec, mod_w, mod_b)
    mods = mods[:, :N_GROUPS].reshape(DEPTH * N_GROUPS * N_MOD, 1, D)

    cos_l, sin_l = _rope_tables()

    def ffn_pair(hp, xn_p, hs, l, g, w13, w2, k0, nxt):
        g = g[l][None, :]
        hid, w2p = _ffn_up(xn_p, w13, l, w2)
        hp, xn_s = _ffn_down(hid, w2p, hp, mods, l, k0 + 2, False, side=(hs, g, l, k0))
        hid = _ffn_up(xn_s, w13, l)
        if nxt is None:
            return hp, None, _ffn_down(hid, w2p, hs, mods, l, k0 + 2, True)
        hs, xn_p = _ffn_down(hid, w2p, hs, mods, l, k0 + 2, True, side=(hp,) + nxt)
        return hp, xn_p, hs

    xn_p = _norm_mod(hp, norm_ffn1[0][None, :], mods, 0, 0, False)
    new_ak, new_av, new_bk, new_bv = [], [], [], []
    for l in range(DEPTH):
        g_mix = norm_mix[l][None, :]
        hp, xn_p, hs = ffn_pair(hp, xn_p, hs, l, norm_ffn1, ffn1_w13, ffn1_w2, 0, (g_mix, l, 3))

        nrm = jnp.concatenate([a_q_norm[l][None], a_k_norm[l][None], b_q_norm[l][None], b_k_norm[l][None],
                               jnp.zeros((4, HD), F32)], axis=0)
        cw = jnp.concatenate([conv_w[l], jnp.zeros((5, CONV_CH), F32)], axis=0)
        cb = conv_b[l][None, :]
        sink = a_sink[l]

        pp = _in_proj(xn_p, w_in, l)
        qa, ka, va, qb, kb, vb, cak, cav, cbk, cbv = _prep_ctx(pp, nrm)
        new_ak.append(cak.reshape(BATCH, SEQ, N_KV_A, HD))
        new_av.append(cav.reshape(BATCH, SEQ, N_KV_A, HD))
        new_bk.append(cbk.reshape(BATCH, SEQ, N_H_B, HD))
        new_bv.append(cbv.reshape(BATCH, SEQ, N_H_B, HD))
        o_ab = _attn_ctx(sink, qa, ka, va, qb, kb, vb)
        o_c = _short_conv(pp, cw, cb, SEQ, CONV_CH)
        hp = _out_proj([o_ab, o_c], w_out, hp, mods, l, False)

        ps = _in_proj(_norm_mod(hs, g_mix, mods, l, 3, True), w_in, l)
        qa, ka, va, qb, kb, vb = _prep_lat(ps, nrm, cos_l, sin_l)
        o_a = _attn_win(sink, qa, ka, va,
                        cache_a_k[:, l].reshape(DEC_BATCH, PAST, W_AKV).astype(BF),
                        cache_a_v[:, l].reshape(DEC_BATCH, PAST, W_AKV).astype(BF))
        o_b = _attn_nbr(qb, kb, vb,
                        cache_b_k[:, l].reshape(DEC_BATCH, PAST, W_B).astype(BF),
                        cache_b_v[:, l].reshape(DEC_BATCH, PAST, W_B).astype(BF),
                        _nbr_bias_table(b_rpb[l]))
        o_c = _short_conv(ps, cw, cb, DEC_SEQ, 128)
        hs = _out_proj([o_a, o_b, o_c], w_out, hs, mods, l, True)

        xn_p = _norm_mod(hp, norm_ffn2[l][None, :], mods, l, 6, False)
        nxt = (norm_ffn1[l + 1][None, :], l + 1, 0) if l + 1 < DEPTH else None
        hp, xn_p, hs = ffn_pair(hp, xn_p, hs, l, norm_ffn2, ffn2_w13, ffn2_w2, 6, nxt)

    return (hp.reshape(BATCH, SEQ, D), hs.reshape(DEC_BATCH, DEC_SEQ, D),
            jnp.stack(new_ak, axis=1), jnp.stack(new_av, axis=1),
            jnp.stack(new_bk, axis=1), jnp.stack(new_bv, axis=1))
```

```python
import functools

import jax
import jax.numpy as jnp
import numpy as np
from jax import lax
from jax.experimental import pallas as pl
from jax.experimental.pallas import tpu as pltpu

D = 4096
BATCH = 16
SEQ = 256
DEPTH = 2
DEC_BATCH = 2
DEC_SEQ = 4096
PAST = 512
GRID_W = 64
GRID_H = DEC_SEQ // GRID_W
HD = 128
N_Q_A = 12
N_KV_A = 4
GQA = N_Q_A // N_KV_A
WINDOW = 128
N_H_B = 12
NA_ROWS = 8
NA_COLS = 16
CONV_CH = 1024
D_FF = 11008
N_MOD = 9
ROPE_BASE = 10000.0
EPS = 1e-6
NEG_INF = -1e30
SCALE = HD ** -0.5

TP = BATCH * SEQ
TS = DEC_BATCH * DEC_SEQ
N_GROUPS = 1 + DEC_BATCH
W_AQ = N_Q_A * HD
W_AKV = N_KV_A * HD
W_B = N_H_B * HD
QKV_W = W_AQ + 2 * W_AKV + 3 * W_B
IN_W = QKV_W + 3 * CONV_CH
MIX_W = W_AQ + W_B + CONV_CH

FF_BLK = 256
N_FF_BLK = D_FF // FF_BLK
FF_PAD = (N_FF_BLK + 1) * FF_BLK

VMEM_LIMIT = 56 * 1024 * 1024
VMEM_LIMIT_LHS2 = 62 * 1024 * 1024

BF = jnp.bfloat16
F32 = jnp.float32


def _cparams(sem, vmem=VMEM_LIMIT):
    return pltpu.CompilerParams(dimension_semantics=sem, vmem_limit_bytes=vmem)


def _dot(a, b):
    return jnp.dot(a, b, preferred_element_type=F32)


def _dot_nt(a, b):
    return lax.dot_general(a, b, (((1,), (1,)), ((), ())), preferred_element_type=F32)


def _mod_index(layer, k, rows_per_block, latent):
    base = layer * N_GROUPS * N_MOD + k
    if not latent:
        return lambda i: base
    blocks_per_group = DEC_SEQ // rows_per_block
    return lambda i: base + (1 + i // blocks_per_group) * N_MOD


MOD_TN = 512
MOD_ROWS = 16


def _mod_kernel(c_ref, w_ref, b_ref, o_ref):
    c = c_ref[...]
    x = (c * jax.nn.sigmoid(c)).astype(BF)
    o_ref[...] = _dot(x, w_ref[...].astype(BF)) + b_ref[...]


def _modulation(cvec, mod_w, mod_b):
    n = N_MOD * D
    return pl.pallas_call(
        _mod_kernel,
        out_shape=jax.ShapeDtypeStruct((DEPTH, MOD_ROWS, n), F32),
        grid=(DEPTH, n // MOD_TN),
        in_specs=[
            pl.BlockSpec((MOD_ROWS, D), lambda l, j: (0, 0)),
            pl.BlockSpec((None, D, MOD_TN), lambda l, j: (l, 0, j)),
            pl.BlockSpec((None, 1, MOD_TN), lambda l, j: (l, 0, j)),
        ],
        out_specs=pl.BlockSpec((None, MOD_ROWS, MOD_TN), lambda l, j: (l, 0, j)),
        compiler_params=_cparams(("parallel", "parallel")),
        name="modulation",
    )(cvec, mod_w, mod_b.reshape(DEPTH, 1, n))


NORM_TM = 512


def _norm_kernel(h_ref, g_ref, shift_ref, scale_ref, o_ref):
    x = h_ref[...]
    ms = jnp.mean(x * x, axis=-1, keepdims=True)
    y = x * lax.rsqrt(ms + EPS) * g_ref[...]
    o_ref[...] = (y * (1.0 + scale_ref[...]) + shift_ref[...]).astype(BF)


def _norm_mod(h, g, mods, layer, k_shift, latent):
    rows = h.shape[0]
    ish = _mod_index(layer, k_shift, NORM_TM, latent)
    isc = _mod_index(layer, k_shift + 1, NORM_TM, latent)
    return pl.pallas_call(
        _norm_kernel,
        out_shape=jax.ShapeDtypeStruct((rows, D), BF),
        grid=(rows // NORM_TM,),
        in_specs=[
            pl.BlockSpec((NORM_TM, D), lambda i: (i, 0)),
            pl.BlockSpec((1, D), lambda i: (0, 0)),
            pl.BlockSpec((None, 1, D), lambda i: (ish(i), 0, 0)),
            pl.BlockSpec((None, 1, D), lambda i: (isc(i), 0, 0)),
        ],
        out_specs=pl.BlockSpec((NORM_TM, D), lambda i: (i, 0)),
        compiler_params=_cparams(("parallel",)),
        name="norm_mod",
    )(h, g, mods, mods)


MM_TM = 2048


def _resident_lhs_spec(tm, width):
    return pl.BlockSpec((tm, width), lambda i, j: (i, 0), pipeline_mode=pl.Buffered(1))


def _ffn_up_kernel(x_ref, w1_ref, w3_ref, *rest, w2_slabs):
    if w2_slabs:
        w2_ref, o_ref, w2p_ref = rest
    else:
        (o_ref,) = rest
    j = pl.program_id(1)

    @pl.when(j < N_FF_BLK)
    def _():
        x = x_ref[...]
        a = _dot(x, w1_ref[...].astype(BF))
        b = _dot(x, w3_ref[...].astype(BF))
        o_ref[...] = (a * jax.nn.sigmoid(a) * b).astype(BF)

    @pl.when(j == N_FF_BLK)
    def _():
        o_ref[...] = jnp.zeros_like(o_ref)

    if w2_slabs:
        s = pl.program_id(0) * pl.num_programs(1) + j

        @pl.when(s < w2_slabs)
        def _():
            w2p_ref[...] = w2_ref[...].astype(BF)

        @pl.when(s >= w2_slabs)
        def _():
            w2p_ref[...] = jnp.zeros_like(w2p_ref)


def _ffn_up(xn, w13, layer, w2=None):
    rows = xn.shape[0]
    last = N_FF_BLK - 1
    n_j = N_FF_BLK + 1
    in_specs = [
        pl.BlockSpec((MM_TM, D), lambda i, j: (i, 0)),
        pl.BlockSpec((None, D, FF_BLK), lambda i, j: (layer, 0, jnp.minimum(j, last))),
        pl.BlockSpec((None, D, FF_BLK), lambda i, j: (layer, 0, N_FF_BLK + jnp.minimum(j, last))),
    ]
    out_shape = [jax.ShapeDtypeStruct((rows, FF_PAD), BF)]
    out_specs = [pl.BlockSpec((MM_TM, FF_BLK), lambda i, j: (i, j))]
    args = [xn, w13, w13]
    w2_slabs = 0
    if w2 is not None:
        n_steps = (rows // MM_TM) * n_j
        assert FF_PAD % n_steps == 0
        slab = FF_PAD // n_steps
        assert D_FF % slab == 0 and slab % 16 == 0
        w2_slabs = D_FF // slab
        in_specs.append(pl.BlockSpec((None, slab, D), lambda i, j: (layer, jnp.minimum(i * n_j + j, w2_slabs - 1), 0)))
        out_shape.append(jax.ShapeDtypeStruct((FF_PAD, D), BF))
        out_specs.append(pl.BlockSpec((slab, D), lambda i, j: (i * n_j + j, 0)))
        args.append(w2)
    outs = pl.pallas_call(
        functools.partial(_ffn_up_kernel, w2_slabs=w2_slabs),
        out_shape=out_shape,
        grid=(rows // MM_TM, n_j),
        in_specs=in_specs,
        out_specs=out_specs,
        compiler_params=_cparams(("arbitrary", "arbitrary"), VMEM_LIMIT_LHS2),
        name="ffn_up",
    )(*args)
    return outs if w2 is not None else outs[0]


DOWN_TM = 1024
DOWN_TN = 1024
DOWN_TK = FF_PAD // 4


SIDE_ROWS = 128


def _ffn_down_kernel(x_ref, w_ref, h_ref, gate_ref, *rest, side_every_k):
    if side_every_k is None:
        (o_ref,) = rest
    else:
        sh_ref, sg_ref, sshift_ref, sscale_ref, o_ref, sxn_ref = rest
    k = pl.program_id(2)
    last = pl.num_programs(2) - 1

    def side(first_k):
        if side_every_k or (first_k and side_every_k is not None):
            _norm_kernel(sh_ref, sg_ref, sshift_ref, sscale_ref, sxn_ref)

    @pl.when(k == 0)
    def _():
        o_ref[...] = _dot(x_ref[...], w_ref[...])
        side(True)

    @pl.when(jnp.logical_and(k > 0, k < last))
    def _():
        o_ref[...] += _dot(x_ref[...], w_ref[...])
        side(False)

    @pl.when(k == last)
    def _():
        acc = o_ref[...] + _dot(x_ref[...], w_ref[...])
        o_ref[...] = h_ref[...] + (0.5 * gate_ref[...]) * acc
        side(False)


def _ffn_down(x, w2p, h, mods, layer, k_gate, latent, side=None):
    rows = x.shape[0]
    n_j, n_k = D // DOWN_TN, FF_PAD // DOWN_TK
    ig = _mod_index(layer, k_gate, DOWN_TM, latent)
    in_specs = [
        pl.BlockSpec((DOWN_TM, DOWN_TK), lambda i, j, k: (i, k)),
        pl.BlockSpec((DOWN_TK, DOWN_TN), lambda i, j, k: (k, j)),
        pl.BlockSpec((DOWN_TM, DOWN_TN), lambda i, j, k: (i, j)),
        pl.BlockSpec((None, 1, DOWN_TN), lambda i, j, k: (ig(i), 0, j)),
    ]
    out_shape = [jax.ShapeDtypeStruct((rows, D), F32)]
    out_specs = [pl.BlockSpec((DOWN_TM, DOWN_TN), lambda i, j, k: (i, j))]
    args = [x, w2p, h, mods]
    side_every_k = None
    if side is not None:
        h_other, g, s_layer, k_shift = side
        n_chunks = h_other.shape[0] // SIDE_ROWS
        n_ij = (rows // DOWN_TM) * n_j
        side_every_k = n_chunks == n_ij * n_k
        assert side_every_k or n_chunks == n_ij
        if side_every_k:
            def chunk(i, j, k):
                return (i * n_j + j) * n_k + k
        else:
            def chunk(i, j, k):
                return i * n_j + j
        ish = _mod_index(s_layer, k_shift, SIDE_ROWS, not latent)
        isc = _mod_index(s_layer, k_shift + 1, SIDE_ROWS, not latent)
        in_specs += [
            pl.BlockSpec((SIDE_ROWS, D), lambda i, j, k: (chunk(i, j, k), 0)),
            pl.BlockSpec((1, D), lambda i, j, k: (0, 0)),
            pl.BlockSpec((None, 1, D), lambda i, j, k: (ish(chunk(i, j, k)), 0, 0)),
            pl.BlockSpec((None, 1, D), lambda i, j, k: (isc(chunk(i, j, k)), 0, 0)),
        ]
        out_shape.append(jax.ShapeDtypeStruct(h_other.shape, BF))
        out_specs.append(pl.BlockSpec((SIDE_ROWS, D), lambda i, j, k: (chunk(i, j, k), 0)))
        args += [h_other, g, mods, mods]
    outs = pl.pallas_call(
        functools.partial(_ffn_down_kernel, side_every_k=side_every_k),
        out_shape=out_shape,
        grid=(rows // DOWN_TM, n_j, n_k),
        in_specs=in_specs,
        out_specs=out_specs,
        compiler_params=_cparams(("arbitrary", "arbitrary", "arbitrary")),
        name="ffn_down",
    )(*args)
    return outs if side is not None else outs[0]


IN_TN = 512


def _in_proj_kernel(x_ref, w_ref, o_ref):
    o_ref[...] = _dot(x_ref[...], w_ref[...].astype(BF))


def _in_proj(u, w_in, layer):
    rows = u.shape[0]
    return pl.pallas_call(
        _in_proj_kernel,
        out_shape=jax.ShapeDtypeStruct((rows, IN_W), F32),
        grid=(rows // MM_TM, IN_W // IN_TN),
        in_specs=[
            pl.BlockSpec((MM_TM, D), lambda i, j: (i, 0)),
            pl.BlockSpec((None, D, IN_TN), lambda i, j: (layer, 0, j)),
        ],
        out_specs=pl.BlockSpec((MM_TM, IN_TN), lambda i, j: (i, j)),
        compiler_params=_cparams(("parallel", "arbitrary"), VMEM_LIMIT_LHS2),
        name="in_proj",
    )(u, w_in)


OUT_TN = 512


def _out_proj_kernel(*refs, n_lhs):
    xs, ws = refs[:n_lhs], refs[n_lhs:2 * n_lhs]
    h_ref, gate_ref, o_ref = refs[2 * n_lhs:]
    acc = _dot(xs[0][...], ws[0][...].astype(BF))
    for x_ref, w_ref in zip(xs[1:], ws[1:]):
        acc += _dot(x_ref[...], w_ref[...].astype(BF))
    o_ref[...] = h_ref[...] + gate_ref[...] * acc


def _out_proj(xs, w_out, h, mods, layer, latent):
    rows = h.shape[0]
    ig = _mod_index(layer, 5, MM_TM, latent)
    x_specs, w_specs, c0 = [], [], 0
    for x in xs:
        w = x.shape[1]
        assert c0 % w == 0
        x_specs.append(_resident_lhs_spec(MM_TM, w))
        w_specs.append(pl.BlockSpec((None, w, OUT_TN), functools.partial(lambda i, j, r: (layer, r, j), r=c0 // w)))
        c0 += w
    assert c0 == MIX_W
    return pl.pallas_call(
        functools.partial(_out_proj_kernel, n_lhs=len(xs)),
        out_shape=jax.ShapeDtypeStruct((rows, D), F32),
        grid=(rows // MM_TM, D // OUT_TN),
        in_specs=x_specs + w_specs + [
            pl.BlockSpec((MM_TM, OUT_TN), lambda i, j: (i, j)),
            pl.BlockSpec((None, 1, OUT_TN), lambda i, j: (ig(i), 0, j)),
        ],
        out_specs=pl.BlockSpec((MM_TM, OUT_TN), lambda i, j: (i, j)),
        compiler_params=_cparams(("parallel", "arbitrary")),
        name="out_proj",
    )(*xs, *([w_out] * len(xs)), h, mods)


PREP_TM = 256
C_QA = 0
C_KA = C_QA + W_AQ
C_VA = C_KA + W_AKV
C_QB = C_VA + W_AKV
C_KB = C_QB + W_B
C_VB = C_KB + W_B


def _head_rms(x, g):
    ms = jnp.mean(x * x, axis=-1, keepdims=True)
    return x * lax.rsqrt(ms + EPS) * g


def _rope(x, cos, sin_signed, lo_lane):
    swapped = jnp.where(lo_lane, pltpu.roll(x, HD - 32, axis=1), pltpu.roll(x, 32, axis=1))
    return x * cos + swapped * sin_signed


def _prep_body(p_ref, nrm_ref, qa_ref, ka_ref, va_ref, qb_ref, kb_ref, vb_ref, rope, caches):
    g_aq = nrm_ref[0:1, :]
    g_ak = nrm_ref[1:2, :]
    g_bq = nrm_ref[2:3, :]
    g_bk = nrm_ref[3:4, :]
    if rope is not None:
        cos, sin_signed = rope
        lo_lane = (lax.broadcasted_iota(jnp.int32, (PREP_TM, HD), 1) % 64) < 32

    def col(c0, h):
        return p_ref[:, c0 + h * HD: c0 + (h + 1) * HD]

    for h in range(N_Q_A):
        y = _head_rms(col(C_QA, h), g_aq)
        if rope is not None:
            y = _rope(y, cos, sin_signed, lo_lane)
        qa_ref[:, h * HD:(h + 1) * HD] = y.astype(BF)
    for h in range(N_KV_A):
        y = _head_rms(col(C_KA, h), g_ak)
        if caches is not None:
            caches[0][:, h * HD:(h + 1) * HD] = y
        if rope is not None:
            y = _rope(y, cos, sin_signed, lo_lane)
        ka_ref[:, h * HD:(h + 1) * HD] = y.astype(BF)
    va = p_ref[:, C_VA:C_VA + W_AKV]
    va_ref[...] = va.astype(BF)
    if caches is not None:
        caches[1][...] = va
    for h in range(N_H_B):
        qb_ref[:, h * HD:(h + 1) * HD] = _head_rms(col(C_QB, h), g_bq).astype(BF)
        y = _head_rms(col(C_KB, h), g_bk)
        if caches is not None:
            caches[2][:, h * HD:(h + 1) * HD] = y
        kb_ref[:, h * HD:(h + 1) * HD] = y.astype(BF)
    vb = p_ref[:, C_VB:C_VB + W_B]
    vb_ref[...] = vb.astype(BF)
    if caches is not None:
        caches[3][...] = vb


def _prep_ctx_kernel(p_ref, nrm_ref, qa_ref, ka_ref, va_ref, qb_ref, kb_ref, vb_ref,
                     cak_ref, cav_ref, cbk_ref, cbv_ref):
    _prep_body(p_ref, nrm_ref, qa_ref, ka_ref, va_ref, qb_ref, kb_ref, vb_ref,
               None, (cak_ref, cav_ref, cbk_ref, cbv_ref))


def _prep_lat_kernel(p_ref, nrm_ref, cos_ref, sin_ref, qa_ref, ka_ref, va_ref, qb_ref, kb_ref, vb_ref):
    _prep_body(p_ref, nrm_ref, qa_ref, ka_ref, va_ref, qb_ref, kb_ref, vb_ref,
               (cos_ref[...], sin_ref[...]), None)


def _qkv_specs(rows):
    widths = (W_AQ, W_AKV, W_AKV, W_B, W_B, W_B)
    shapes = [jax.ShapeDtypeStruct((rows, w), BF) for w in widths]
    specs = [pl.BlockSpec((PREP_TM, w), lambda i: (i, 0)) for w in widths]
    return shapes, specs


def _prep_ctx(p, nrm):
    shapes, specs = _qkv_specs(TP)
    cache_w = (W_AKV, W_AKV, W_B, W_B)
    shapes += [jax.ShapeDtypeStruct((TP, w), F32) for w in cache_w]
    specs += [pl.BlockSpec((PREP_TM, w), lambda i: (i, 0)) for w in cache_w]
    return pl.pallas_call(
        _prep_ctx_kernel,
        out_shape=shapes,
        grid=(TP // PREP_TM,),
        in_specs=[
            pl.BlockSpec((PREP_TM, QKV_W), lambda i: (i, 0)),
            pl.BlockSpec((8, HD), lambda i: (0, 0)),
        ],
        out_specs=specs,
        compiler_params=_cparams(("parallel",)),
        name="prep_ctx",
    )(p, nrm)


def _prep_lat(p, nrm, cos, sin_signed):
    shapes, specs = _qkv_specs(TS)
    per_seq = DEC_SEQ // PREP_TM
    return pl.pallas_call(
        _prep_lat_kernel,
        out_shape=shapes,
        grid=(TS // PREP_TM,),
        in_specs=[
            pl.BlockSpec((PREP_TM, QKV_W), lambda i: (i, 0)),
            pl.BlockSpec((8, HD), lambda i: (0, 0)),
            pl.BlockSpec((PREP_TM, HD), lambda i: (i % per_seq, 0)),
            pl.BlockSpec((PREP_TM, HD), lambda i: (i % per_seq, 0)),
        ],
        out_specs=specs,
        compiler_params=_cparams(("parallel",)),
        name="prep_lat",
    )(p, nrm, cos, sin_signed)


def _conv_kernel(xc_ref, gb_ref, gc_ref, w_ref, b_ref, o_ref):
    u = gc_ref[...] * xc_ref[...]
    n = u.shape[0]
    row = lax.broadcasted_iota(jnp.int32, u.shape, 0)
    prev = jnp.where(row == 0, 0.0, pltpu.roll(u, 1, axis=0))
    nxt = jnp.where(row == n - 1, 0.0, pltpu.roll(u, n - 1, axis=0))
    y = w_ref[0:1, :] * prev + w_ref[1:2, :] * u + w_ref[2:3, :] * nxt + b_ref[...]
    o_ref[...] = (gb_ref[...] * y).astype(BF)


def _short_conv(p, conv_w, conv_b, rows, ch):
    n_seq = p.shape[0] // rows
    c0 = QKV_W // ch
    per = CONV_CH // ch
    return pl.pallas_call(
        _conv_kernel,
        out_shape=jax.ShapeDtypeStruct((n_seq * rows, CONV_CH), BF),
        grid=(n_seq, per),
        in_specs=[
            pl.BlockSpec((rows, ch), lambda s, c: (s, c0 + c)),
            pl.BlockSpec((rows, ch), lambda s, c: (s, c0 + per + c)),
            pl.BlockSpec((rows, ch), lambda s, c: (s, c0 + 2 * per + c)),
            pl.BlockSpec((8, ch), lambda s, c: (0, c)),
            pl.BlockSpec((1, ch), lambda s, c: (0, c)),
        ],
        out_specs=pl.BlockSpec((rows, ch), lambda s, c: (s, c)),
        compiler_params=_cparams(("parallel", "parallel")),
        name="short_conv",
    )(p, p, p, conv_w, conv_b)


def _softmax_parts(parts, sink):
    m = parts[0].max(axis=-1, keepdims=True)
    for s in parts[1:]:
        m = jnp.maximum(m, s.max(axis=-1, keepdims=True))
    if sink is not None:
        m = jnp.maximum(m, sink)
    es = [jnp.exp(s - m) for s in parts]
    den = es[0].sum(axis=-1, keepdims=True)
    for e in es[1:]:
        den = den + e.sum(axis=-1, keepdims=True)
    if sink is not None:
        den = den + jnp.exp(sink - m)
    inv = 1.0 / den
    return [(e * inv).astype(BF) for e in es]


def _sink_column(sink_ref, kv, rows):
    return jnp.concatenate(
        [jnp.full((rows, 1), sink_ref[kv * GQA + g], F32) for g in range(GQA)], axis=0)


def _group_rows(x, kv):
    return jnp.concatenate([x[:, (kv * GQA + g) * HD:(kv * GQA + g + 1) * HD] for g in range(GQA)], axis=0)


def _ungroup_rows(o, rows):
    return [o[g * rows:(g + 1) * rows] for g in range(GQA)]


def _attn_ctx_kernel(sink_ref, qa_ref, ka_ref, va_ref, qb_ref, kb_ref, vb_ref, o_ref):
    qa, ka, va = qa_ref[...], ka_ref[...], va_ref[...]
    qb, kb, vb = qb_ref[...], kb_ref[...], vb_ref[...]
    sls = [slice(h * HD, (h + 1) * HD) for h in range(N_H_B)]
    s_a = [_dot_nt(_group_rows(qa, kv), ka[:, sls[kv]]) * SCALE for kv in range(N_KV_A)]
    s_b = [_dot_nt(qb[:, sl], kb[:, sl]) * SCALE for sl in sls]
    p_a = [_softmax_parts([s], _sink_column(sink_ref, kv, SEQ))[0] for kv, s in enumerate(s_a)]
    p_b = [_softmax_parts([s], None)[0] for s in s_b]
    outs = []
    for kv, p in enumerate(p_a):
        outs += _ungroup_rows(_dot(p, va[:, sls[kv]]), SEQ)
    outs += [_dot(p, vb[:, sl]) for p, sl in zip(p_b, sls)]
    o_ref[...] = jnp.concatenate(outs, axis=1).astype(BF)


def _attn_ctx(sink, qa, ka, va, qb, kb, vb):
    def spec(w):
        return pl.BlockSpec((SEQ, w), lambda b: (b, 0))
    return pl.pallas_call(
        _attn_ctx_kernel,
        out_shape=jax.ShapeDtypeStruct((TP, W_AQ + W_B), BF),
        grid=(BATCH,),
        in_specs=[pl.BlockSpec(memory_space=pltpu.SMEM),
                  spec(W_AQ), spec(W_AKV), spec(W_AKV), spec(W_B), spec(W_B), spec(W_B)],
        out_specs=spec(W_AQ + W_B),
        compiler_params=_cparams(("parallel",)),
        name="attn_ctx",
    )(sink, qa, ka, va, qb, kb, vb)


QB = 128
SPAN = QB + 2 * WINDOW


WIN_BLOCKS = 2


def _attn_win_kernel(sink_ref, q_ref, k_ref, v_ref, ck_ref, cv_ref, o_ref):
    t = pl.program_id(1)
    ck, cv = ck_ref[...], cv_ref[...]
    sls = [slice(kv * HD, (kv + 1) * HD) for kv in range(N_KV_A)]
    qs, kks, vvs, valids = [], [], [], []
    for u in range(WIN_BLOCKS):
        start = (t * WIN_BLOCKS + u) * QB
        s0 = pl.multiple_of(jnp.clip(start - WINDOW, 0, DEC_SEQ - SPAN), QB)
        qpos = start + lax.broadcasted_iota(jnp.int32, (GQA * QB, SPAN), 0) % QB
        kpos = s0 + lax.broadcasted_iota(jnp.int32, (GQA * QB, SPAN), 1)
        valids.append(jnp.abs(qpos - kpos) <= WINDOW)
        qs.append(q_ref[u * QB:(u + 1) * QB, :])
        kks.append(jnp.concatenate([k_ref[pl.ds(s0, SPAN), :], ck], axis=0))
        vvs.append(jnp.concatenate([v_ref[pl.ds(s0, SPAN), :], cv], axis=0))
    ss = [[_dot_nt(_group_rows(q, kv), kk[:, sl]) * SCALE for kv, sl in enumerate(sls)] for q, kk in zip(qs, kks)]
    ss = [[jnp.concatenate([jnp.where(valid, s[:, :SPAN], NEG_INF), s[:, SPAN:]], axis=1) for s in row]
          for row, valid in zip(ss, valids)]
    ps = [[_softmax_parts([s], _sink_column(sink_ref, kv, QB))[0] for kv, s in enumerate(row)] for row in ss]
    blocks = []
    for row, vv in zip(ps, vvs):
        outs = []
        for p, sl in zip(row, sls):
            outs += _ungroup_rows(_dot(p, vv[:, sl]), QB)
        blocks.append(jnp.concatenate(outs, axis=1))
    o_ref[...] = jnp.concatenate(blocks, axis=0).astype(BF)


def _attn_win(sink, qa, ka, va, ck, cv):
    nb = DEC_SEQ // (QB * WIN_BLOCKS)
    return pl.pallas_call(
        _attn_win_kernel,
        out_shape=jax.ShapeDtypeStruct((TS, W_AQ), BF),
        grid=(DEC_BATCH, nb),
        in_specs=[
            pl.BlockSpec(memory_space=pltpu.SMEM),
            pl.BlockSpec((QB * WIN_BLOCKS, W_AQ), lambda b, i: (b * nb + i, 0)),
            pl.BlockSpec((DEC_SEQ, W_AKV), lambda b, i: (b, 0)),
            pl.BlockSpec((DEC_SEQ, W_AKV), lambda b, i: (b, 0)),
            pl.BlockSpec((None, PAST, W_AKV), lambda b, i: (b, 0, 0)),
            pl.BlockSpec((None, PAST, W_AKV), lambda b, i: (b, 0, 0)),
        ],
        out_specs=pl.BlockSpec((QB * WIN_BLOCKS, W_AQ), lambda b, i: (b * nb + i, 0)),
        compiler_params=_cparams(("parallel", "arbitrary")),
        name="attn_win",
    )(sink, qa, ka, va, ck, cv)


NB_HG = 12
NB_KEYS = NA_ROWS * GRID_W


def _row_start(r):
    return jnp.clip(r - NA_ROWS // 2, 0, GRID_H - NA_ROWS)


NB_ROWS = 2


def _attn_nbr_kernel(q_ref, k_ref, v_ref, ck_ref, cv_ref, *rest):
    bias_refs, o_ref = rest[:NB_ROWS], rest[NB_ROWS]
    t = pl.program_id(2)
    ck, cv = ck_ref[...], cv_ref[...]
    sls = [slice(h * HD, (h + 1) * HD) for h in range(NB_HG)]
    qs, kks, vvs = [], [], []
    for u in range(NB_ROWS):
        k0 = pl.multiple_of(_row_start(t * NB_ROWS + u) * GRID_W, GRID_W)
        qs.append(q_ref[u * GRID_W:(u + 1) * GRID_W, :])
        kks.append(jnp.concatenate([k_ref[pl.ds(k0, NB_KEYS), :], ck], axis=0))
        vvs.append(jnp.concatenate([v_ref[pl.ds(k0, NB_KEYS), :], cv], axis=0))
    ss = [[_dot_nt(q[:, sl], kk[:, sl]) * SCALE for sl in sls] for q, kk in zip(qs, kks)]
    ss = [[jnp.concatenate([s[:, :NB_KEYS] + bias_refs[u][h], s[:, NB_KEYS:]], axis=1) for h, s in enumerate(row)]
          for u, row in enumerate(ss)]
    ps = [[_softmax_parts([s], None)[0] for s in row] for row in ss]
    outs = [jnp.concatenate([_dot(p, vv[:, sl]) for p, sl in zip(row, sls)], axis=1) for row, vv in zip(ps, vvs)]
    o_ref[...] = jnp.concatenate(outs, axis=0).astype(BF)


def _attn_nbr(qb, kb, vb, ck, cv, bias):
    hw = NB_HG * HD
    n_hg = N_H_B // NB_HG
    steps = GRID_H // NB_ROWS
    rows = NB_ROWS * GRID_W

    def bias_spec(u):
        def index(b, g, t):
            r = t * NB_ROWS + u
            return (r - _row_start(r), g, 0, 0)
        return pl.BlockSpec((None, NB_HG, GRID_W, NB_KEYS), index)

    return pl.pallas_call(
        _attn_nbr_kernel,
        out_shape=jax.ShapeDtypeStruct((TS, W_B), BF),
        grid=(DEC_BATCH, n_hg, steps),
        in_specs=[
            pl.BlockSpec((rows, hw), lambda b, g, t: (b * steps + t, g)),
            pl.BlockSpec((DEC_SEQ, hw), lambda b, g, t: (b, g), pipeline_mode=pl.Buffered(1)),
            pl.BlockSpec((DEC_SEQ, hw), lambda b, g, t: (b, g), pipeline_mode=pl.Buffered(1)),
            pl.BlockSpec((None, PAST, hw), lambda b, g, t: (b, 0, g)),
            pl.BlockSpec((None, PAST, hw), lambda b, g, t: (b, 0, g)),
        ] + [bias_spec(u) for u in range(NB_ROWS)],
        out_specs=pl.BlockSpec((rows, hw), lambda b, g, t: (b * steps + t, g)),
        compiler_params=_cparams(("parallel", "parallel", "arbitrary")),
        name="attn_nbr",
    )(qb, kb, vb, ck, cv, *([bias] * NB_ROWS))


def _rope_tables():
    pairs = HD // 4
    t = np.arange(DEC_SEQ)
    pos = np.stack([t // GRID_W, t % GRID_W], axis=-1).astype(np.float32)
    inv = jnp.asarray(ROPE_BASE, F32) ** (-jnp.arange(pairs, dtype=F32) * 2.0 / (HD // 2))
    ang = jnp.asarray(pos)[:, :, None] * inv
    cos = jnp.cos(ang)
    sin = jnp.sin(ang)
    cos_l = jnp.concatenate([cos, cos], axis=-1).reshape(DEC_SEQ, HD)
    sin_l = jnp.concatenate([-sin, sin], axis=-1).reshape(DEC_SEQ, HD)
    return cos_l, sin_l


def _nbr_bias_table(rpb):
    edge = GRID_W - NA_COLS
    ext = jnp.concatenate([jnp.repeat(rpb[..., :1], edge, axis=-1), rpb.astype(F32),
                           jnp.repeat(rpb[..., -1:], edge, axis=-1)], axis=-1)
    toep = jnp.stack([ext[..., GRID_W - 1 - q: 2 * GRID_W - 1 - q] for q in range(GRID_W)], axis=-2)
    qc = np.arange(GRID_W)
    cs = np.clip(qc - NA_COLS // 2, 0, GRID_W - NA_COLS)
    col_ok = (qc[None, :] >= cs[:, None]) & (qc[None, :] < cs[:, None] + NA_COLS)
    toep = jnp.where(jnp.asarray(col_ok), toep, NEG_INF)
    tabs = []
    for s in range(NA_ROWS):
        blk = toep[:, NA_ROWS - 1 - s: 2 * NA_ROWS - 1 - s]
        tabs.append(jnp.transpose(blk, (0, 2, 1, 3)).reshape(N_H_B, GRID_W, NB_KEYS))
    return jnp.stack(tabs, axis=0)


def kernel(x_prompt, x_sample, cache_a_k, cache_a_v, cache_b_k, cache_b_v, c, c_ctx,
           mod_w, mod_b, norm_ffn1, norm_mix, norm_ffn2, ffn1_w13, ffn1_w2, ffn2_w13, ffn2_w2,
           w_in, w_out, a_q_norm, a_k_norm, a_sink, b_q_norm, b_k_norm, b_rpb, conv_w, conv_b):
    hp = x_prompt.reshape(TP, D)
    hs = x_sample.reshape(TS, D)

    cvec = jnp.concatenate([c_ctx[None, :], c, jnp.zeros((MOD_ROWS - N_GROUPS, D), F32)], axis=0)
    mods = _modulation(cvec, mod_w, mod_b)
    mods = mods[:, :N_GROUPS].reshape(DEPTH * N_GROUPS * N_MOD, 1, D)

    cos_l, sin_l = _rope_tables()

    def ffn_pair(hp, xn_p, hs, l, g, w13, w2, k0, nxt):
        g = g[l][None, :]
        hid, w2p = _ffn_up(xn_p, w13, l, w2)
        hp, xn_s = _ffn_down(hid, w2p, hp, mods, l, k0 + 2, False, side=(hs, g, l, k0))
        hid = _ffn_up(xn_s, w13, l)
        if nxt is None:
            return hp, None, _ffn_down(hid, w2p, hs, mods, l, k0 + 2, True)
        hs, xn_p = _ffn_down(hid, w2p, hs, mods, l, k0 + 2, True, side=(hp,) + nxt)
        return hp, xn_p, hs

    xn_p = _norm_mod(hp, norm_ffn1[0][None, :], mods, 0, 0, False)
    new_ak, new_av, new_bk, new_bv = [], [], [], []
    for l in range(DEPTH):
        g_mix = norm_mix[l][None, :]
        hp, xn_p, hs = ffn_pair(hp, xn_p, hs, l, norm_ffn1, ffn1_w13, ffn1_w2, 0, (g_mix, l, 3))

        nrm = jnp.concatenate([a_q_norm[l][None], a_k_norm[l][None], b_q_norm[l][None], b_k_norm[l][None],
                               jnp.zeros((4, HD), F32)], axis=0)
        cw = jnp.concatenate([conv_w[l], jnp.zeros((5, CONV_CH), F32)], axis=0)
        cb = conv_b[l][None, :]
        sink = a_sink[l]

        pp = _in_proj(xn_p, w_in, l)
        qa, ka, va, qb, kb, vb, cak, cav, cbk, cbv = _prep_ctx(pp, nrm)
        new_ak.append(cak.reshape(BATCH, SEQ, N_KV_A, HD))
        new_av.append(cav.reshape(BATCH, SEQ, N_KV_A, HD))
        new_bk.append(cbk.reshape(BATCH, SEQ, N_H_B, HD))
        new_bv.append(cbv.reshape(BATCH, SEQ, N_H_B, HD))
        o_ab = _attn_ctx(sink, qa, ka, va, qb, kb, vb)
        o_c = _short_conv(pp, cw, cb, SEQ, CONV_CH)
        hp = _out_proj([o_ab, o_c], w_out, hp, mods, l, False)

        ps = _in_proj(_norm_mod(hs, g_mix, mods, l, 3, True), w_in, l)
        qa, ka, va, qb, kb, vb = _prep_lat(ps, nrm, cos_l, sin_l)
        o_a = _attn_win(sink, qa, ka, va,
                        cache_a_k[:, l].reshape(DEC_BATCH, PAST, W_AKV).astype(BF),
                        cache_a_v[:, l].reshape(DEC_BATCH, PAST, W_AKV).astype(BF))
        o_b = _attn_nbr(qb, kb, vb,
                        cache_b_k[:, l].reshape(DEC_BATCH, PAST, W_B).astype(BF),
                        cache_b_v[:, l].reshape(DEC_BATCH, PAST, W_B).astype(BF),
                        _nbr_bias_table(b_rpb[l]))
        o_c = _short_conv(ps, cw, cb, DEC_SEQ, 128)
        hs = _out_proj([o_a, o_b, o_c], w_out, hs, mods, l, True)

        xn_p = _norm_mod(hp, norm_ffn2[l][None, :], mods, l, 6, False)
        nxt = (norm_ffn1[l + 1][None, :], l + 1, 0) if l + 1 < DEPTH else None
        hp, xn_p, hs = ffn_pair(hp, xn_p, hs, l, norm_ffn2, ffn2_w13, ffn2_w2, 6, nxt)

    return (hp.reshape(BATCH, SEQ, D), hs.reshape(DEC_BATCH, DEC_SEQ, D),
            jnp.stack(new_ak, axis=1), jnp.stack(new_av, axis=1),
            jnp.stack(new_bk, axis=1), jnp.stack(new_bv, axis=1))
```

```python
import functools

import jax
import jax.numpy as jnp
import numpy as np
from jax import lax
from jax.experimental import pallas as pl
from jax.experimental.pallas import tpu as pltpu

D = 4096
BATCH = 16
SEQ = 256
DEPTH = 2
DEC_BATCH = 2
DEC_SEQ = 4096
PAST = 512
GRID_W = 64
GRID_H = DEC_SEQ // GRID_W
HD = 128
N_Q_A = 12
N_KV_A = 4
GQA = N_Q_A // N_KV_A
WINDOW = 128
N_H_B = 12
NA_ROWS = 8
NA_COLS = 16
CONV_CH = 1024
D_FF = 11008
N_MOD = 9
ROPE_BASE = 10000.0
EPS = 1e-6
NEG_INF = -1e30
SCALE = HD ** -0.5
LOG2E = 1.4426950408889634
SCALE_LOG2 = SCALE * LOG2E

TP = BATCH * SEQ
TS = DEC_BATCH * DEC_SEQ
N_GROUPS = 1 + DEC_BATCH
W_AQ = N_Q_A * HD
W_AKV = N_KV_A * HD
W_B = N_H_B * HD
QKV_W = W_AQ + 2 * W_AKV + 3 * W_B
IN_W = QKV_W + 3 * CONV_CH
MIX_W = W_AQ + W_B + CONV_CH

FF_BLK = 256
N_FF_BLK = D_FF // FF_BLK
FF_PAD = (N_FF_BLK + 1) * FF_BLK

VMEM_LIMIT = 56 * 1024 * 1024
VMEM_LIMIT_LHS2 = 62 * 1024 * 1024

BF = jnp.bfloat16
F32 = jnp.float32


def _cparams(sem, vmem=VMEM_LIMIT):
    return pltpu.CompilerParams(dimension_semantics=sem, vmem_limit_bytes=vmem)


def _dot(a, b):
    return jnp.dot(a, b, preferred_element_type=F32)


def _dot_nt(a, b):
    return lax.dot_general(a, b, (((1,), (1,)), ((), ())), preferred_element_type=F32)


def _mod_index(layer, k, rows_per_block, latent):
    base = layer * N_GROUPS * N_MOD + k
    if not latent:
        return lambda i: base
    blocks_per_group = DEC_SEQ // rows_per_block
    return lambda i: base + (1 + i // blocks_per_group) * N_MOD


MOD_TN = 512
MOD_ROWS = 16


def _mod_kernel(c_ref, w_ref, b_ref, o_ref):
    c = c_ref[...]
    x = (c * jax.nn.sigmoid(c)).astype(BF)
    o_ref[...] = _dot(x, w_ref[...].astype(BF)) + b_ref[...]


def _modulation(cvec, mod_w, mod_b):
    n = N_MOD * D
    return pl.pallas_call(
        _mod_kernel,
        out_shape=jax.ShapeDtypeStruct((DEPTH, MOD_ROWS, n), F32),
        grid=(DEPTH, n // MOD_TN),
        in_specs=[
            pl.BlockSpec((MOD_ROWS, D), lambda l, j: (0, 0)),
            pl.BlockSpec((None, D, MOD_TN), lambda l, j: (l, 0, j)),
            pl.BlockSpec((None, 1, MOD_TN), lambda l, j: (l, 0, j)),
        ],
        out_specs=pl.BlockSpec((None, MOD_ROWS, MOD_TN), lambda l, j: (l, 0, j)),
        compiler_params=_cparams(("parallel", "parallel")),
        name="modulation",
    )(cvec, mod_w, mod_b.reshape(DEPTH, 1, n))


NORM_TM = 512


def _norm_kernel(h_ref, g_ref, shift_ref, scale_ref, o_ref):
    x = h_ref[...]
    ms = jnp.mean(x * x, axis=-1, keepdims=True)
    y = x * lax.rsqrt(ms + EPS) * g_ref[...]
    o_ref[...] = (y * (1.0 + scale_ref[...]) + shift_ref[...]).astype(BF)


def _norm_mod(h, g, mods, layer, k_shift, latent):
    rows = h.shape[0]
    ish = _mod_index(layer, k_shift, NORM_TM, latent)
    isc = _mod_index(layer, k_shift + 1, NORM_TM, latent)
    return pl.pallas_call(
        _norm_kernel,
        out_shape=jax.ShapeDtypeStruct((rows, D), BF),
        grid=(rows // NORM_TM,),
        in_specs=[
            pl.BlockSpec((NORM_TM, D), lambda i: (i, 0)),
            pl.BlockSpec((1, D), lambda i: (0, 0)),
            pl.BlockSpec((None, 1, D), lambda i: (ish(i), 0, 0)),
            pl.BlockSpec((None, 1, D), lambda i: (isc(i), 0, 0)),
        ],
        out_specs=pl.BlockSpec((NORM_TM, D), lambda i: (i, 0)),
        compiler_params=_cparams(("parallel",)),
        name="norm_mod",
    )(h, g, mods, mods)


MM_TM = 2048


def _resident_lhs_spec(tm, width):
    return pl.BlockSpec((tm, width), lambda i, j: (i, 0), pipeline_mode=pl.Buffered(1))


def _ffn_up_kernel(x_ref, w1_ref, w3_ref, *rest, w2_slabs):
    if w2_slabs:
        w2_ref, o_ref, w2p_ref = rest
    else:
        (o_ref,) = rest
    j = pl.program_id(1)

    @pl.when(j < N_FF_BLK)
    def _():
        x = x_ref[...]
        a = _dot(x, w1_ref[...].astype(BF))
        b = _dot(x, w3_ref[...].astype(BF))
        o_ref[...] = (a * jax.nn.sigmoid(a) * b).astype(BF)

    @pl.when(j == N_FF_BLK)
    def _():
        o_ref[...] = jnp.zeros_like(o_ref)

    if w2_slabs:
        s = pl.program_id(0) * pl.num_programs(1) + j

        @pl.when(s < w2_slabs)
        def _():
            w2p_ref[...] = w2_ref[...].astype(BF)

        @pl.when(s >= w2_slabs)
        def _():
            w2p_ref[...] = jnp.zeros_like(w2p_ref)


def _ffn_up(xn, w13, layer, w2=None):
    rows = xn.shape[0]
    last = N_FF_BLK - 1
    n_j = N_FF_BLK + 1
    in_specs = [
        pl.BlockSpec((MM_TM, D), lambda i, j: (i, 0)),
        pl.BlockSpec((None, D, FF_BLK), lambda i, j: (layer, 0, jnp.minimum(j, last))),
        pl.BlockSpec((None, D, FF_BLK), lambda i, j: (layer, 0, N_FF_BLK + jnp.minimum(j, last))),
    ]
    out_shape = [jax.ShapeDtypeStruct((rows, FF_PAD), BF)]
    out_specs = [pl.BlockSpec((MM_TM, FF_BLK), lambda i, j: (i, j))]
    args = [xn, w13, w13]
    w2_slabs = 0
    if w2 is not None:
        n_steps = (rows // MM_TM) * n_j
        assert FF_PAD % n_steps == 0
        slab = FF_PAD // n_steps
        assert D_FF % slab == 0 and slab % 16 == 0
        w2_slabs = D_FF // slab
        in_specs.append(pl.BlockSpec((None, slab, D), lambda i, j: (layer, jnp.minimum(i * n_j + j, w2_slabs - 1), 0)))
        out_shape.append(jax.ShapeDtypeStruct((FF_PAD, D), BF))
        out_specs.append(pl.BlockSpec((slab, D), lambda i, j: (i * n_j + j, 0)))
        args.append(w2)
    outs = pl.pallas_call(
        functools.partial(_ffn_up_kernel, w2_slabs=w2_slabs),
        out_shape=out_shape,
        grid=(rows // MM_TM, n_j),
        in_specs=in_specs,
        out_specs=out_specs,
        compiler_params=_cparams(("arbitrary", "arbitrary"), VMEM_LIMIT_LHS2),
        name="ffn_up",
    )(*args)
    return outs if w2 is not None else outs[0]


DOWN_TM = 1024
DOWN_TN = 1024
DOWN_TK = FF_PAD // 4


SIDE_ROWS = 128


def _ffn_down_kernel(x_ref, w_ref, h_ref, gate_ref, *rest, side_every_k):
    if side_every_k is None:
        (o_ref,) = rest
    else:
        sh_ref, sg_ref, sshift_ref, sscale_ref, o_ref, sxn_ref = rest
    k = pl.program_id(2)
    last = pl.num_programs(2) - 1

    def side(first_k):
        if side_every_k or (first_k and side_every_k is not None):
            _norm_kernel(sh_ref, sg_ref, sshift_ref, sscale_ref, sxn_ref)

    @pl.when(k == 0)
    def _():
        o_ref[...] = _dot(x_ref[...], w_ref[...])
        side(True)

    @pl.when(jnp.logical_and(k > 0, k < last))
    def _():
        o_ref[...] += _dot(x_ref[...], w_ref[...])
        side(False)

    @pl.when(k == last)
    def _():
        acc = o_ref[...] + _dot(x_ref[...], w_ref[...])
        o_ref[...] = h_ref[...] + (0.5 * gate_ref[...]) * acc
        side(False)


def _ffn_down(x, w2p, h, mods, layer, k_gate, latent, side=None):
    rows = x.shape[0]
    n_j, n_k = D // DOWN_TN, FF_PAD // DOWN_TK
    ig = _mod_index(layer, k_gate, DOWN_TM, latent)
    in_specs = [
        pl.BlockSpec((DOWN_TM, DOWN_TK), lambda i, j, k: (i, k)),
        pl.BlockSpec((DOWN_TK, DOWN_TN), lambda i, j, k: (k, j)),
        pl.BlockSpec((DOWN_TM, DOWN_TN), lambda i, j, k: (i, j)),
        pl.BlockSpec((None, 1, DOWN_TN), lambda i, j, k: (ig(i), 0, j)),
    ]
    out_shape = [jax.ShapeDtypeStruct((rows, D), F32)]
    out_specs = [pl.BlockSpec((DOWN_TM, DOWN_TN), lambda i, j, k: (i, j))]
    args = [x, w2p, h, mods]
    side_every_k = None
    if side is not None:
        h_other, g, s_layer, k_shift = side
        n_chunks = h_other.shape[0] // SIDE_ROWS
        n_ij = (rows // DOWN_TM) * n_j
        side_every_k = n_chunks == n_ij * n_k
        assert side_every_k or n_chunks == n_ij
        if side_every_k:
            def chunk(i, j, k):
                return (i * n_j + j) * n_k + k
        else:
            def chunk(i, j, k):
                return i * n_j + j
        ish = _mod_index(s_layer, k_shift, SIDE_ROWS, not latent)
        isc = _mod_index(s_layer, k_shift + 1, SIDE_ROWS, not latent)
        in_specs += [
            pl.BlockSpec((SIDE_ROWS, D), lambda i, j, k: (chunk(i, j, k), 0)),
            pl.BlockSpec((1, D), lambda i, j, k: (0, 0)),
            pl.BlockSpec((None, 1, D), lambda i, j, k: (ish(chunk(i, j, k)), 0, 0)),
            pl.BlockSpec((None, 1, D), lambda i, j, k: (isc(chunk(i, j, k)), 0, 0)),
        ]
        out_shape.append(jax.ShapeDtypeStruct(h_other.shape, BF))
        out_specs.append(pl.BlockSpec((SIDE_ROWS, D), lambda i, j, k: (chunk(i, j, k), 0)))
        args += [h_other, g, mods, mods]
    outs = pl.pallas_call(
        functools.partial(_ffn_down_kernel, side_every_k=side_every_k),
        out_shape=out_shape,
        grid=(rows // DOWN_TM, n_j, n_k),
        in_specs=in_specs,
        out_specs=out_specs,
        compiler_params=_cparams(("arbitrary", "arbitrary", "arbitrary")),
        name="ffn_down",
    )(*args)
    return outs if side is not None else outs[0]


IN_TN = 512


def _in_proj_kernel(x_ref, w_ref, o_ref):
    o_ref[...] = _dot(x_ref[...], w_ref[...].astype(BF))


def _in_proj(u, w_in, layer):
    rows = u.shape[0]
    return pl.pallas_call(
        _in_proj_kernel,
        out_shape=jax.ShapeDtypeStruct((rows, IN_W), F32),
        grid=(rows // MM_TM, IN_W // IN_TN),
        in_specs=[
            pl.BlockSpec((MM_TM, D), lambda i, j: (i, 0)),
            pl.BlockSpec((None, D, IN_TN), lambda i, j: (layer, 0, j)),
        ],
        out_specs=pl.BlockSpec((MM_TM, IN_TN), lambda i, j: (i, j)),
        compiler_params=_cparams(("parallel", "arbitrary"), VMEM_LIMIT_LHS2),
        name="in_proj",
    )(u, w_in)


OUT_TN = 512


def _out_proj_kernel(*refs, n_lhs):
    xs, ws = refs[:n_lhs], refs[n_lhs:2 * n_lhs]
    h_ref, gate_ref, o_ref = refs[2 * n_lhs:]
    acc = _dot(xs[0][...], ws[0][...].astype(BF))
    for x_ref, w_ref in zip(xs[1:], ws[1:]):
        acc += _dot(x_ref[...], w_ref[...].astype(BF))
    o_ref[...] = h_ref[...] + gate_ref[...] * acc


def _out_proj(xs, w_out, h, mods, layer, latent):
    rows = h.shape[0]
    ig = _mod_index(layer, 5, MM_TM, latent)
    x_specs, w_specs, c0 = [], [], 0
    for x in xs:
        w = x.shape[1]
        assert c0 % w == 0
        x_specs.append(_resident_lhs_spec(MM_TM, w))
        w_specs.append(pl.BlockSpec((None, w, OUT_TN), functools.partial(lambda i, j, r: (layer, r, j), r=c0 // w)))
        c0 += w
    assert c0 == MIX_W
    return pl.pallas_call(
        functools.partial(_out_proj_kernel, n_lhs=len(xs)),
        out_shape=jax.ShapeDtypeStruct((rows, D), F32),
        grid=(rows // MM_TM, D // OUT_TN),
        in_specs=x_specs + w_specs + [
            pl.BlockSpec((MM_TM, OUT_TN), lambda i, j: (i, j)),
            pl.BlockSpec((None, 1, OUT_TN), lambda i, j: (ig(i), 0, j)),
        ],
        out_specs=pl.BlockSpec((MM_TM, OUT_TN), lambda i, j: (i, j)),
        compiler_params=_cparams(("parallel", "arbitrary")),
        name="out_proj",
    )(*xs, *([w_out] * len(xs)), h, mods)


PREP_TM = 256
C_QA = 0
C_KA = C_QA + W_AQ
C_VA = C_KA + W_AKV
C_QB = C_VA + W_AKV
C_KB = C_QB + W_B
C_VB = C_KB + W_B


def _head_rms(x, g):
    ms = jnp.mean(x * x, axis=-1, keepdims=True)
    return x * lax.rsqrt(ms + EPS) * g


def _rope(x, cos, sin_signed, lo_lane):
    swapped = jnp.where(lo_lane, pltpu.roll(x, HD - 32, axis=1), pltpu.roll(x, 32, axis=1))
    return x * cos + swapped * sin_signed


def _prep_body(p_ref, nrm_ref, qa_ref, ka_ref, va_ref, qb_ref, kb_ref, vb_ref, rope, caches):
    g_aq = nrm_ref[0:1, :]
    g_ak = nrm_ref[1:2, :]
    g_bq = nrm_ref[2:3, :]
    g_bk = nrm_ref[3:4, :]
    if rope is not None:
        cos, sin_signed = rope
        lo_lane = (lax.broadcasted_iota(jnp.int32, (PREP_TM, HD), 1) % 64) < 32

    def col(c0, h):
        return p_ref[:, c0 + h * HD: c0 + (h + 1) * HD]

    for h in range(N_Q_A):
        y = _head_rms(col(C_QA, h), g_aq)
        if rope is not None:
            y = _rope(y, cos, sin_signed, lo_lane)
        qa_ref[:, h * HD:(h + 1) * HD] = y.astype(BF)
    for h in range(N_KV_A):
        y = _head_rms(col(C_KA, h), g_ak)
        if caches is not None:
            caches[0][:, h * HD:(h + 1) * HD] = y
        if rope is not None:
            y = _rope(y, cos, sin_signed, lo_lane)
        ka_ref[:, h * HD:(h + 1) * HD] = y.astype(BF)
    va = p_ref[:, C_VA:C_VA + W_AKV]
    va_ref[...] = va.astype(BF)
    if caches is not None:
        caches[1][...] = va
    for h in range(N_H_B):
        qb_ref[:, h * HD:(h + 1) * HD] = _head_rms(col(C_QB, h), g_bq).astype(BF)
        y = _head_rms(col(C_KB, h), g_bk)
        if caches is not None:
            caches[2][:, h * HD:(h + 1) * HD] = y
        kb_ref[:, h * HD:(h + 1) * HD] = y.astype(BF)
    vb = p_ref[:, C_VB:C_VB + W_B]
    vb_ref[...] = vb.astype(BF)
    if caches is not None:
        caches[3][...] = vb


def _prep_ctx_kernel(p_ref, nrm_ref, qa_ref, ka_ref, va_ref, qb_ref, kb_ref, vb_ref,
                     cak_ref, cav_ref, cbk_ref, cbv_ref):
    _prep_body(p_ref, nrm_ref, qa_ref, ka_ref, va_ref, qb_ref, kb_ref, vb_ref,
               None, (cak_ref, cav_ref, cbk_ref, cbv_ref))


def _prep_lat_kernel(p_ref, nrm_ref, cos_ref, sin_ref, qa_ref, ka_ref, va_ref, qb_ref, kb_ref, vb_ref):
    _prep_body(p_ref, nrm_ref, qa_ref, ka_ref, va_ref, qb_ref, kb_ref, vb_ref,
               (cos_ref[...], sin_ref[...]), None)


def _qkv_specs(rows):
    widths = (W_AQ, W_AKV, W_AKV, W_B, W_B, W_B)
    shapes = [jax.ShapeDtypeStruct((rows, w), BF) for w in widths]
    specs = [pl.BlockSpec((PREP_TM, w), lambda i: (i, 0)) for w in widths]
    return shapes, specs


def _prep_ctx(p, nrm):
    shapes, specs = _qkv_specs(TP)
    cache_w = (W_AKV, W_AKV, W_B, W_B)
    shapes += [jax.ShapeDtypeStruct((TP, w), F32) for w in cache_w]
    specs += [pl.BlockSpec((PREP_TM, w), lambda i: (i, 0)) for w in cache_w]
    return pl.pallas_call(
        _prep_ctx_kernel,
        out_shape=shapes,
        grid=(TP // PREP_TM,),
        in_specs=[
            pl.BlockSpec((PREP_TM, QKV_W), lambda i: (i, 0)),
            pl.BlockSpec((8, HD), lambda i: (0, 0)),
        ],
        out_specs=specs,
        compiler_params=_cparams(("parallel",)),
        name="prep_ctx",
    )(p, nrm)


def _prep_lat(p, nrm, cos, sin_signed):
    shapes, specs = _qkv_specs(TS)
    per_seq = DEC_SEQ // PREP_TM
    return pl.pallas_call(
        _prep_lat_kernel,
        out_shape=shapes,
        grid=(TS // PREP_TM,),
        in_specs=[
            pl.BlockSpec((PREP_TM, QKV_W), lambda i: (i, 0)),
            pl.BlockSpec((8, HD), lambda i: (0, 0)),
            pl.BlockSpec((PREP_TM, HD), lambda i: (i % per_seq, 0)),
            pl.BlockSpec((PREP_TM, HD), lambda i: (i % per_seq, 0)),
        ],
        out_specs=specs,
        compiler_params=_cparams(("parallel",)),
        name="prep_lat",
    )(p, nrm, cos, sin_signed)


def _conv_kernel(xc_ref, gb_ref, gc_ref, w_ref, b_ref, o_ref):
    u = gc_ref[...] * xc_ref[...]
    n = u.shape[0]
    row = lax.broadcasted_iota(jnp.int32, u.shape, 0)
    prev = jnp.where(row == 0, 0.0, pltpu.roll(u, 1, axis=0))
    nxt = jnp.where(row == n - 1, 0.0, pltpu.roll(u, n - 1, axis=0))
    y = w_ref[0:1, :] * prev + w_ref[1:2, :] * u + w_ref[2:3, :] * nxt + b_ref[...]
    o_ref[...] = (gb_ref[...] * y).astype(BF)


def _short_conv(p, conv_w, conv_b, rows, ch):
    n_seq = p.shape[0] // rows
    c0 = QKV_W // ch
    per = CONV_CH // ch
    return pl.pallas_call(
        _conv_kernel,
        out_shape=jax.ShapeDtypeStruct((n_seq * rows, CONV_CH), BF),
        grid=(n_seq, per),
        in_specs=[
            pl.BlockSpec((rows, ch), lambda s, c: (s, c0 + c)),
            pl.BlockSpec((rows, ch), lambda s, c: (s, c0 + per + c)),
            pl.BlockSpec((rows, ch), lambda s, c: (s, c0 + 2 * per + c)),
            pl.BlockSpec((8, ch), lambda s, c: (0, c)),
            pl.BlockSpec((1, ch), lambda s, c: (0, c)),
        ],
        out_specs=pl.BlockSpec((rows, ch), lambda s, c: (s, c)),
        compiler_params=_cparams(("parallel", "parallel")),
        name="short_conv",
    )(p, p, p, conv_w, conv_b)


def _softmax_parts(parts, sink):
    m = parts[0].max(axis=-1, keepdims=True)
    for s in parts[1:]:
        m = jnp.maximum(m, s.max(axis=-1, keepdims=True))
    if sink is not None:
        m = jnp.maximum(m, sink)
    es = [jnp.exp2(s - m) for s in parts]
    den = es[0].sum(axis=-1, keepdims=True)
    for e in es[1:]:
        den = den + e.sum(axis=-1, keepdims=True)
    if sink is not None:
        den = den + jnp.exp2(sink - m)
    inv = 1.0 / den
    return [(e * inv).astype(BF) for e in es]


def _sink_column(sink_ref, kv, rows):
    return jnp.concatenate(
        [jnp.full((rows, 1), sink_ref[kv * GQA + g] * LOG2E, F32) for g in range(GQA)], axis=0)


def _group_rows(x, kv):
    return jnp.concatenate([x[:, (kv * GQA + g) * HD:(kv * GQA + g + 1) * HD] for g in range(GQA)], axis=0)


def _ungroup_rows(o, rows):
    return [o[g * rows:(g + 1) * rows] for g in range(GQA)]


CTX_SEQS = 2


def _attn_ctx_kernel(sink_ref, qa_ref, ka_ref, va_ref, qb_ref, kb_ref, vb_ref, o_ref):
    sls = [slice(h * HD, (h + 1) * HD) for h in range(N_H_B)]
    rows = [slice(u * SEQ, (u + 1) * SEQ) for u in range(CTX_SEQS)]
    qa, ka, va = ([r[rs, :] for rs in rows] for r in (qa_ref, ka_ref, va_ref))
    qb, kb, vb = ([r[rs, :] for rs in rows] for r in (qb_ref, kb_ref, vb_ref))
    s_a = [[_dot_nt(_group_rows(qa[u], kv), ka[u][:, sls[kv]]) * SCALE_LOG2 for kv in range(N_KV_A)]
           for u in range(CTX_SEQS)]
    s_b = [[_dot_nt(qb[u][:, sl], kb[u][:, sl]) * SCALE_LOG2 for sl in sls] for u in range(CTX_SEQS)]
    p_a = [[_softmax_parts([s], _sink_column(sink_ref, kv, SEQ))[0] for kv, s in enumerate(row)] for row in s_a]
    p_b = [[_softmax_parts([s], None)[0] for s in row] for row in s_b]
    blocks = []
    for u in range(CTX_SEQS):
        outs = []
        for kv, p in enumerate(p_a[u]):
            outs += _ungroup_rows(_dot(p, va[u][:, sls[kv]]), SEQ)
        outs += [_dot(p, vb[u][:, sl]) for p, sl in zip(p_b[u], sls)]
        blocks.append(jnp.concatenate(outs, axis=1))
    o_ref[...] = jnp.concatenate(blocks, axis=0).astype(BF)


def _attn_ctx(sink, qa, ka, va, qb, kb, vb):
    def spec(w):
        return pl.BlockSpec((CTX_SEQS * SEQ, w), lambda b: (b, 0))
    return pl.pallas_call(
        _attn_ctx_kernel,
        out_shape=jax.ShapeDtypeStruct((TP, W_AQ + W_B), BF),
        grid=(BATCH // CTX_SEQS,),
        in_specs=[pl.BlockSpec(memory_space=pltpu.SMEM),
                  spec(W_AQ), spec(W_AKV), spec(W_AKV), spec(W_B), spec(W_B), spec(W_B)],
        out_specs=spec(W_AQ + W_B),
        compiler_params=_cparams(("parallel",)),
        name="attn_ctx",
    )(sink, qa, ka, va, qb, kb, vb)


QB = 128
SPAN = QB + 2 * WINDOW


WIN_BLOCKS = 2


def _attn_win_kernel(sink_ref, q_ref, k_ref, v_ref, ck_ref, cv_ref, o_ref):
    t = pl.program_id(1)
    ck, cv = ck_ref[...], cv_ref[...]
    sls = [slice(kv * HD, (kv + 1) * HD) for kv in range(N_KV_A)]
    qs, kks, vvs, valids = [], [], [], []
    for u in range(WIN_BLOCKS):
        start = (t * WIN_BLOCKS + u) * QB
        s0 = pl.multiple_of(jnp.clip(start - WINDOW, 0, DEC_SEQ - SPAN), QB)
        qpos = start + lax.broadcasted_iota(jnp.int32, (GQA * QB, SPAN), 0) % QB
        kpos = s0 + lax.broadcasted_iota(jnp.int32, (GQA * QB, SPAN), 1)
        valids.append(jnp.abs(qpos - kpos) <= WINDOW)
        qs.append(q_ref[u * QB:(u + 1) * QB, :])
        kks.append(jnp.concatenate([k_ref[pl.ds(s0, SPAN), :], ck], axis=0))
        vvs.append(jnp.concatenate([v_ref[pl.ds(s0, SPAN), :], cv], axis=0))
    ss = [[_dot_nt(_group_rows(q, kv), kk[:, sl]) * SCALE_LOG2 for kv, sl in enumerate(sls)] for q, kk in zip(qs, kks)]
    ss = [[jnp.concatenate([jnp.where(valid, s[:, :SPAN], NEG_INF), s[:, SPAN:]], axis=1) for s in row]
          for row, valid in zip(ss, valids)]
    ps = [[_softmax_parts([s], _sink_column(sink_ref, kv, QB))[0] for kv, s in enumerate(row)] for row in ss]
    blocks = []
    for row, vv in zip(ps, vvs):
        outs = []
        for p, sl in zip(row, sls):
            outs += _ungroup_rows(_dot(p, vv[:, sl]), QB)
        blocks.append(jnp.concatenate(outs, axis=1))
    o_ref[...] = jnp.concatenate(blocks, axis=0).astype(BF)


def _attn_win(sink, qa, ka, va, ck, cv):
    nb = DEC_SEQ // (QB * WIN_BLOCKS)
    return pl.pallas_call(
        _attn_win_kernel,
        out_shape=jax.ShapeDtypeStruct((TS, W_AQ), BF),
        grid=(DEC_BATCH, nb),
        in_specs=[
            pl.BlockSpec(memory_space=pltpu.SMEM),
            pl.BlockSpec((QB * WIN_BLOCKS, W_AQ), lambda b, i: (b * nb + i, 0)),
            pl.BlockSpec((DEC_SEQ, W_AKV), lambda b, i: (b, 0)),
            pl.BlockSpec((DEC_SEQ, W_AKV), lambda b, i: (b, 0)),
            pl.BlockSpec((None, PAST, W_AKV), lambda b, i: (b, 0, 0)),
            pl.BlockSpec((None, PAST, W_AKV), lambda b, i: (b, 0, 0)),
        ],
        out_specs=pl.BlockSpec((QB * WIN_BLOCKS, W_AQ), lambda b, i: (b * nb + i, 0)),
        compiler_params=_cparams(("parallel", "arbitrary")),
        name="attn_win",
    )(sink, qa, ka, va, ck, cv)


NB_HG = 12
NB_KEYS = NA_ROWS * GRID_W


def _row_start(r):
    return jnp.clip(r - NA_ROWS // 2, 0, GRID_H - NA_ROWS)


NB_ROWS = 2


def _attn_nbr_kernel(q_ref, k_ref, v_ref, ck_ref, cv_ref, *rest):
    bias_refs, o_ref = rest[:NB_ROWS], rest[NB_ROWS]
    t = pl.program_id(2)
    ck, cv = ck_ref[...], cv_ref[...]
    sls = [slice(h * HD, (h + 1) * HD) for h in range(NB_HG)]
    qs, kks, vvs = [], [], []
    for u in range(NB_ROWS):
        k0 = pl.multiple_of(_row_start(t * NB_ROWS + u) * GRID_W, GRID_W)
        qs.append(q_ref[u * GRID_W:(u + 1) * GRID_W, :])
        kks.append(jnp.concatenate([k_ref[pl.ds(k0, NB_KEYS), :], ck], axis=0))
        vvs.append(jnp.concatenate([v_ref[pl.ds(k0, NB_KEYS), :], cv], axis=0))
    ss = [[_dot_nt(q[:, sl], kk[:, sl]) * SCALE_LOG2 for sl in sls] for q, kk in zip(qs, kks)]
    ss = [[jnp.concatenate([s[:, :NB_KEYS] + bias_refs[u][h], s[:, NB_KEYS:]], axis=1) for h, s in enumerate(row)]
          for u, row in enumerate(ss)]
    ps = [[_softmax_parts([s], None)[0] for s in row] for row in ss]
    outs = [jnp.concatenate([_dot(p, vv[:, sl]) for p, sl in zip(row, sls)], axis=1) for row, vv in zip(ps, vvs)]
    o_ref[...] = jnp.concatenate(outs, axis=0).astype(BF)


def _attn_nbr(qb, kb, vb, ck, cv, bias):
    hw = NB_HG * HD
    n_hg = N_H_B // NB_HG
    steps = GRID_H // NB_ROWS
    rows = NB_ROWS * GRID_W

    def bias_spec(u):
        def index(b, g, t):
            r = t * NB_ROWS + u
            return (r - _row_start(r), g, 0, 0)
        return pl.BlockSpec((None, NB_HG, GRID_W, NB_KEYS), index)

    return pl.pallas_call(
        _attn_nbr_kernel,
        out_shape=jax.ShapeDtypeStruct((TS, W_B), BF),
        grid=(DEC_BATCH, n_hg, steps),
        in_specs=[
            pl.BlockSpec((rows, hw), lambda b, g, t: (b * steps + t, g)),
            pl.BlockSpec((DEC_SEQ, hw), lambda b, g, t: (b, g), pipeline_mode=pl.Buffered(1)),
            pl.BlockSpec((DEC_SEQ, hw), lambda b, g, t: (b, g), pipeline_mode=pl.Buffered(1)),
            pl.BlockSpec((None, PAST, hw), lambda b, g, t: (b, 0, g)),
            pl.BlockSpec((None, PAST, hw), lambda b, g, t: (b, 0, g)),
        ] + [bias_spec(u) for u in range(NB_ROWS)],
        out_specs=pl.BlockSpec((rows, hw), lambda b, g, t: (b * steps + t, g)),
        compiler_params=_cparams(("parallel", "parallel", "arbitrary")),
        name="attn_nbr",
    )(qb, kb, vb, ck, cv, *([bias] * NB_ROWS))


def _rope_tables():
    pairs = HD // 4
    t = np.arange(DEC_SEQ)
    pos = np.stack([t // GRID_W, t % GRID_W], axis=-1).astype(np.float32)
    inv = jnp.asarray(ROPE_BASE, F32) ** (-jnp.arange(pairs, dtype=F32) * 2.0 / (HD // 2))
    ang = jnp.asarray(pos)[:, :, None] * inv
    cos = jnp.cos(ang)
    sin = jnp.sin(ang)
    cos_l = jnp.concatenate([cos, cos], axis=-1).reshape(DEC_SEQ, HD)
    sin_l = jnp.concatenate([-sin, sin], axis=-1).reshape(DEC_SEQ, HD)
    return cos_l, sin_l


def _nbr_bias_table(rpb):
    edge = GRID_W - NA_COLS
    ext = jnp.concatenate([jnp.repeat(rpb[..., :1], edge, axis=-1), rpb.astype(F32),
                           jnp.repeat(rpb[..., -1:], edge, axis=-1)], axis=-1)
    toep = jnp.stack([ext[..., GRID_W - 1 - q: 2 * GRID_W - 1 - q] for q in range(GRID_W)], axis=-2)
    qc = np.arange(GRID_W)
    cs = np.clip(qc - NA_COLS // 2, 0, GRID_W - NA_COLS)
    col_ok = (qc[None, :] >= cs[:, None]) & (qc[None, :] < cs[:, None] + NA_COLS)
    toep = jnp.where(jnp.asarray(col_ok), toep * LOG2E, NEG_INF)
    tabs = []
    for s in range(NA_ROWS):
        blk = toep[:, NA_ROWS - 1 - s: 2 * NA_ROWS - 1 - s]
        tabs.append(jnp.transpose(blk, (0, 2, 1, 3)).reshape(N_H_B, GRID_W, NB_KEYS))
    return jnp.stack(tabs, axis=0)


def kernel(x_prompt, x_sample, cache_a_k, cache_a_v, cache_b_k, cache_b_v, c, c_ctx,
           mod_w, mod_b, norm_ffn1, norm_mix, norm_ffn2, ffn1_w13, ffn1_w2, ffn2_w13, ffn2_w2,
           w_in, w_out, a_q_norm, a_k_norm, a_sink, b_q_norm, b_k_norm, b_rpb, conv_w, conv_b):
    hp = x_prompt.reshape(TP, D)
    hs = x_sample.reshape(TS, D)

    cvec = jnp.concatenate([c_ctx[None, :], c, jnp.zeros((MOD_ROWS - N_GROUPS, D), F32)], axis=0)
    mods = _modulation(cvec, mod_w, mod_b)
    mods = mods[:, :N_GROUPS].reshape(DEPTH * N_GROUPS * N_MOD, 1, D)

    cos_l, sin_l = _rope_tables()

    def ffn_pair(hp, xn_p, hs, l, g, w13, w2, k0, nxt):
        g = g[l][None, :]
        hid, w2p = _ffn_up(xn_p, w13, l, w2)
        hp, xn_s = _ffn_down(hid, w2p, hp, mods, l, k0 + 2, False, side=(hs, g, l, k0))
        hid = _ffn_up(xn_s, w13, l)
        if nxt is None:
            return hp, None, _ffn_down(hid, w2p, hs, mods, l, k0 + 2, True)
        hs, xn_p = _ffn_down(hid, w2p, hs, mods, l, k0 + 2, True, side=(hp,) + nxt)
        return hp, xn_p, hs

    xn_p = _norm_mod(hp, norm_ffn1[0][None, :], mods, 0, 0, False)
    new_ak, new_av, new_bk, new_bv = [], [], [], []
    for l in range(DEPTH):
        g_mix = norm_mix[l][None, :]
        hp, xn_p, hs = ffn_pair(hp, xn_p, hs, l, norm_ffn1, ffn1_w13, ffn1_w2, 0, (g_mix, l, 3))

        nrm = jnp.concatenate([a_q_norm[l][None], a_k_norm[l][None], b_q_norm[l][None], b_k_norm[l][None],
                               jnp.zeros((4, HD), F32)], axis=0)
        cw = jnp.concatenate([conv_w[l], jnp.zeros((5, CONV_CH), F32)], axis=0)
        cb = conv_b[l][None, :]
        sink = a_sink[l]

        pp = _in_proj(xn_p, w_in, l)
        qa, ka, va, qb, kb, vb, cak, cav, cbk, cbv = _prep_ctx(pp, nrm)
        new_ak.append(cak.reshape(BATCH, SEQ, N_KV_A, HD))
        new_av.append(cav.reshape(BATCH, SEQ, N_KV_A, HD))
        new_bk.append(cbk.reshape(BATCH, SEQ, N_H_B, HD))
        new_bv.append(cbv.reshape(BATCH, SEQ, N_H_B, HD))
        o_ab = _attn_ctx(sink, qa, ka, va, qb, kb, vb)
        o_c = _short_conv(pp, cw, cb, SEQ, CONV_CH)
        hp = _out_proj([o_ab, o_c], w_out, hp, mods, l, False)

        ps = _in_proj(_norm_mod(hs, g_mix, mods, l, 3, True), w_in, l)
        qa, ka, va, qb, kb, vb = _prep_lat(ps, nrm, cos_l, sin_l)
        o_a = _attn_win(sink, qa, ka, va,
                        cache_a_k[:, l].reshape(DEC_BATCH, PAST, W_AKV).astype(BF),
                        cache_a_v[:, l].reshape(DEC_BATCH, PAST, W_AKV).astype(BF))
        o_b = _attn_nbr(qb, kb, vb,
                        cache_b_k[:, l].reshape(DEC_BATCH, PAST, W_B).astype(BF),
                        cache_b_v[:, l].reshape(DEC_BATCH, PAST, W_B).astype(BF),
                        _nbr_bias_table(b_rpb[l]))
        o_c = _short_conv(ps, cw, cb, DEC_SEQ, 128)
        hs = _out_proj([o_a, o_b, o_c], w_out, hs, mods, l, True)

        xn_p = _norm_mod(hp, norm_ffn2[l][None, :], mods, l, 6, False)
        nxt = (norm_ffn1[l + 1][None, :], l + 1, 0) if l + 1 < DEPTH else None
        hp, xn_p, hs = ffn_pair(hp, xn_p, hs, l, norm_ffn2, ffn2_w13, ffn2_w2, 6, nxt)

    return (hp.reshape(BATCH, SEQ, D), hs.reshape(DEC_BATCH, DEC_SEQ, D),
            jnp.stack(new_ak, axis=1), jnp.stack(new_av, axis=1),
            jnp.stack(new_bk, axis=1), jnp.stack(new_bv, axis=1))
```
